```python
import math
import jax
import jax.numpy as jnp
from jax import lax
import numpy as np

D_MODEL = 2048
BATCH = 2
SEQ = 8192
DEPTH = 4

N_MIXERS = 3
N_RET_LAYERS = (DEPTH + 2) // 3
N_SWA_LAYERS = (DEPTH + 1) // 3
N_RWKV_LAYERS = DEPTH // 3

RET_HEADS = 8
RET_QK_DIM = D_MODEL // RET_HEADS
RET_V_DIM = 2 * D_MODEL // RET_HEADS
RET_CHUNK = 128
RET_GN_EPS = 1e-5
ROPE_BASE = 10000.0

SWA_HEAD_DIM = 64
SWA_Q_HEADS = D_MODEL // SWA_HEAD_DIM
SWA_KV_HEADS = SWA_Q_HEADS // 8
SWA_WINDOW = 128
SWA_BLOCK = SWA_WINDOW
REL_BUCKETS = 32
REL_MAX_DIST = SWA_WINDOW
NEG_INF = -1e30

RWKV_HEAD_DIM = 64
RWKV_HEADS = D_MODEL // RWKV_HEAD_DIM
RWKV_DECAY_LORA = 96
RWKV_AAA_LORA = 96
RWKV_GATE_LORA = 256
RWKV_GN_EPS = 64e-5

D_FF = 5504
CONV_WIDTH = 3

PLE_DIM = 256

LN_EPS = 1e-5
DEEPNORM_ALPHA = (2.0 * DEPTH) ** 0.25
DEEPNORM_BETA = (8.0 * DEPTH) ** -0.25

kernel_name = 'hybrid_retnet_swa_rwkv7_deepnorm_trunk'


def _layer_norm(x, gain, bias):
    xf = x.astype(jnp.float32)
    mu = jnp.mean(xf, axis=-1, keepdims=True)
    var = jnp.mean(jnp.square(xf - mu), axis=-1, keepdims=True)
    return ((xf - mu) * lax.rsqrt(var + LN_EPS)).astype(x.dtype) * gain + bias


def _head_norm(x, gain, bias, eps):
    xf = x.astype(jnp.float32)
    mu = jnp.mean(xf, axis=-1, keepdims=True)
    var = jnp.mean(jnp.square(xf - mu), axis=-1, keepdims=True)
    y = (xf - mu) * lax.rsqrt(var + eps) * gain.astype(jnp.float32)
    if bias is not None:
        y = y + bias.astype(jnp.float32)
    return y


def _rotary(x, pos):
    d = x.shape[-1]
    inv = 1.0 / (ROPE_BASE ** (jnp.arange(0, d, 2, dtype=jnp.float32) / d))
    ang = pos.astype(jnp.float32)[:, None] * inv[None, :]
    cos = jnp.cos(ang)[None, :, None, :].astype(x.dtype)
    sin = jnp.sin(ang)[None, :, None, :].astype(x.dtype)
    x1, x2 = x[..., : d // 2], x[..., d // 2:]
    return jnp.concatenate([x1 * cos - x2 * sin, x1 * sin + x2 * cos], axis=-1)


def _retention(x, w_in, gn_gain, w_out):
    B, S, _ = x.shape
    H, dk, dv, C = RET_HEADS, RET_QK_DIM, RET_V_DIM, RET_CHUNK
    N = S // C
    proj = x @ w_in
    q, k, v, g = jnp.split(proj, [H * dk, 2 * H * dk, 2 * H * dk + H * dv], axis=-1)
    pos = jnp.arange(S)
    q = _rotary(q.reshape(B, S, H, dk), pos)
    k = _rotary(k.reshape(B, S, H, dk), pos) * (dk ** -0.5)
    v = v.reshape(B, S, H, dv)

    log_gamma = jnp.log(1.0 - 2.0 ** (-5.0 - jnp.arange(H, dtype=jnp.float32)))
    idx = jnp.arange(C, dtype=jnp.float32)
    diff = idx[:, None] - idx[None, :]
    decay_mask = jnp.where(diff >= 0, jnp.exp(log_gamma[:, None, None] * jnp.maximum(diff, 0.0)), 0.0)
    q_decay = jnp.exp(log_gamma[None, :] * (idx[:, None] + 1.0))
    k_decay = jnp.exp(log_gamma[None, :] * (C - 1.0 - idx[:, None]))
    chunk_decay = jnp.exp(log_gamma * C)

    qc = q.reshape(B, N, C, H, dk)
    kc = k.reshape(B, N, C, H, dk)
    vc = v.reshape(B, N, C, H, dv)
    scores = jnp.einsum('bnihd,bnjhd->bnhij', qc, kc) * decay_mask.astype(x.dtype)
    inner = jnp.einsum('bnhij,bnjhe->bnihe', scores, vc)

    def step(R, inp):
        q_n, k_n, v_n = inp
        cross = jnp.einsum('bihd,bhde->bihe', q_n, R) * q_decay[None, :, :, None]
        R = R * chunk_decay[None, :, None, None] + jnp.einsum(
            'bjhd,bjhe->bhde', k_n * k_decay[None, :, :, None], v_n)
        return R, cross

    R0 = jnp.zeros((B, H, dk, dv), jnp.float32)
    to_chunks = lambda t: jnp.transpose(t, (1, 0, 2, 3, 4))
    _, cross = lax.scan(step, R0, (to_chunks(qc), to_chunks(kc), to_chunks(vc)))
    cross = jnp.transpose(cross, (1, 0, 2, 3, 4))

    o = (inner + cross).reshape(B, S, H, dv)
    o = _head_norm(o, gn_gain.reshape(H, dv), None, RET_GN_EPS).astype(x.dtype)
    o = jax.nn.silu(g) * o.reshape(B, S, H * dv)
    return o @ w_out


def _t5_buckets(dist):
    n = np.maximum(dist, 0)
    max_exact = REL_BUCKETS // 2
    large = max_exact + (np.log(np.maximum(n, 1) / max_exact) / np.log(REL_MAX_DIST / max_exact)
                         * (REL_BUCKETS - max_exact)).astype(np.int32)
    large = np.minimum(large, REL_BUCKETS - 1)
    return np.where(n < max_exact, n, large).astype(np.int32)


def _swa_attention(x, w_qkv, sinks, w_out, rel_bias):
    B, S, _ = x.shape
    Hq, Hkv, hd, C = SWA_Q_HEADS, SWA_KV_HEADS, SWA_HEAD_DIM, SWA_BLOCK
    G = Hq // Hkv
    N = S // C
    qkv = x @ w_qkv
    q, k, v = jnp.split(qkv, [Hq * hd, (Hq + Hkv) * hd], axis=-1)
    q = q.reshape(B, N, C, Hkv, G, hd) * (hd ** -0.5)
    k = k.reshape(B, N, C, Hkv, hd)
    v = v.reshape(B, N, C, Hkv, hd)

    def with_prev(t):
        prev = jnp.pad(t, ((0, 0), (1, 0), (0, 0), (0, 0), (0, 0)))[:, :-1]
        return jnp.concatenate([prev, t], axis=2)

    kb, vb = with_prev(k), with_prev(v)

    qi = np.arange(C)[:, None]
    kj = np.arange(2 * C)[None, :]
    dist = qi + C - kj
    in_window = jnp.asarray((dist >= 0) & (dist < SWA_WINDOW))
    not_pad = (jnp.arange(N)[:, None, None] > 0) | jnp.asarray(kj >= C)[None]
    valid = in_window[None] & not_pad
    bias = rel_bias[_t5_buckets(dist)]
    bias = jnp.transpose(bias, (2, 0, 1)).reshape(Hkv, G, C, 2 * C).astype(jnp.float32)

    scores = jnp.einsum('bnihgd,bnjhd->bnhgij', q, kb).astype(jnp.float32) + bias[None, None]
    scores = jnp.where(valid[None, :, None, None], scores, NEG_INF)
    sink = sinks.reshape(Hkv, G).astype(jnp.float32)[None, None, :, :, None, None]
    m = jnp.maximum(jnp.max(scores, axis=-1, keepdims=True), sink)
    pexp = jnp.exp(scores - m)
    denom = jnp.sum(pexp, axis=-1, keepdims=True) + jnp.exp(sink - m)
    probs = (pexp / denom).astype(x.dtype)
    o = jnp.einsum('bnhgij,bnjhd->bnihgd', probs, vb).reshape(B, S, Hq * hd)
    return o @ w_out


def _rwkv7_time_mix(x, mix, w_rkv, w0, w1, w2, a0, a1, a2, g1, g2, k_k, k_a, r_k,
                    gn_gain, gn_bias, w_out):
    B, S, D = x.shape
    H, hd = RWKV_HEADS, RWKV_HEAD_DIM
    x_prev = jnp.pad(x, ((0, 0), (1, 0), (0, 0)))[:, :-1]
    xx = x_prev - x
    xr, xw, xk, xv, xa, xg = [x + xx * mix[i] for i in range(6)]
    r, k, v = jnp.einsum('tbsd,tde->tbse', jnp.stack([xr, xk, xv]), w_rkv)
    log_w = -jax.nn.softplus(-(w0 + jnp.tanh(xw @ w1) @ w2)) - 0.5
    decay = jnp.exp(-jnp.exp(log_w.astype(jnp.float32)))
    a = jax.nn.sigmoid(a0 + (xa @ a1) @ a2)
    g = jax.nn.sigmoid(xg @ g1) @ g2
    kk = (k * k_k).reshape(B, S, H, hd).astype(jnp.float32)
    kk = kk / jnp.maximum(jnp.sqrt(jnp.sum(kk * kk, axis=-1, keepdims=True)), 1e-12)
    k = k * (1.0 + (a - 1.0) * k_a)

    heads = lambda t: t.reshape(B, S, H, hd).astype(jnp.float32)
    r_h, k_h, v_h, a_h, w_h = heads(r), heads(k), heads(v), heads(a), heads(decay)
    a_vec = -kk
    b_vec = kk * a_h

    def step(state, inp):
        r_t, w_t, k_t, v_t, av_t, bv_t = inp
        sa = jnp.einsum('bhvk,bhk->bhv', state, av_t)
        state = (state * w_t[:, :, None, :] + sa[..., None] * bv_t[:, :, None, :]
                 + v_t[..., None] * k_t[:, :, None, :])
        y_t = jnp.einsum('bhvk,bhk->bhv', state, r_t)
        return state, y_t

    to_time = lambda t: jnp.transpose(t, (1, 0, 2, 3))
    S0 = jnp.zeros((B, H, hd, hd), jnp.float32)
    _, y = lax.scan(step, S0, (to_time(r_h), to_time(w_h), to_time(k_h), to_time(v_h),
                               to_time(a_vec), to_time(b_vec)))
    y = jnp.transpose(y, (1, 0, 2, 3))
    y = _head_norm(y, gn_gain.reshape(H, hd), gn_bias.reshape(H, hd), RWKV_GN_EPS)
    bonus = jnp.sum(r_h * k_h * r_k.astype(jnp.float32), axis=-1, keepdims=True) * v_h
    y = (y + bonus).reshape(B, S, D).astype(x.dtype) * g
    return y @ w_out


def _conv_ffn(x, w_up, conv_w, conv_b, w_down):
    S = x.shape[1]
    h = x @ w_up
    hp = jnp.pad(h, ((0, 0), (CONV_WIDTH - 1, 0), (0, 0)))
    hc = conv_b + hp[:, CONV_WIDTH - 1:] * conv_w[CONV_WIDTH - 1]
    for tap in range(CONV_WIDTH - 1):
        hc = hc + hp[:, tap:tap + S] * conv_w[tap]
    u, gate = jnp.split(hc, 2, axis=-1)
    return (jax.nn.silu(gate) * u) @ w_down


def setup_inputs(seed: int = 0) -> dict:
    key = jax.random.key(seed)
    ks = iter(jax.random.split(key, 48))
    nrm = lambda shape: jax.random.normal(next(ks), shape, jnp.float32)
    dense = lambda shape, fan_in, scale=1.0: nrm(shape) * (fan_in ** -0.5) * scale
    D, F = D_MODEL, D_FF
    ret_in_cols = 2 * RET_HEADS * RET_QK_DIM + 2 * RET_HEADS * RET_V_DIM
    swa_in_cols = (SWA_Q_HEADS + 2 * SWA_KV_HEADS) * SWA_HEAD_DIM
    nr, ns, nw = N_RET_LAYERS, N_SWA_LAYERS, N_RWKV_LAYERS
    return {
        'x': nrm((BATCH, SEQ, D)),
        'p': nrm((DEPTH, BATCH, SEQ, PLE_DIM)),
        'ln_gain': 1.0 + 0.02 * nrm((DEPTH, 2, D)),
        'ln_bias': 0.02 * nrm((DEPTH, 2, D)),
        'ret_w_in': dense((nr, D, ret_in_cols), D),
        'ret_gn_gain': 1.0 + 0.02 * nrm((nr, RET_HEADS * RET_V_DIM)),
        'ret_w_out': dense((nr, RET_HEADS * RET_V_DIM, D), RET_HEADS * RET_V_DIM, DEEPNORM_BETA),
        'swa_w_qkv': dense((ns, D, swa_in_cols), D),
        'swa_sinks': nrm((ns, SWA_Q_HEADS)),
        'swa_w_out': dense((ns, SWA_Q_HEADS * SWA_HEAD_DIM, D), SWA_Q_HEADS * SWA_HEAD_DIM, DEEPNORM_BETA),
        'rel_bias': 0.5 * nrm((REL_BUCKETS, SWA_Q_HEADS)),
        'rwkv_mix': jax.random.uniform(next(ks), (nw, 6, D), jnp.float32),
        'rwkv_w_rkv': dense((nw, 3, D, D), D),
        'rwkv_w0': 0.5 * nrm((nw, D)) - 0.5,
        'rwkv_w1': dense((nw, D, RWKV_DECAY_LORA), D),
        'rwkv_w2': dense((nw, RWKV_DECAY_LORA, D), RWKV_DECAY_LORA, 0.5),
        'rwkv_a0': 0.1 * nrm((nw, D)),
        'rwkv_a1': dense((nw, D, RWKV_AAA_LORA), D),
        'rwkv_a2': dense((nw, RWKV_AAA_LORA, D), RWKV_AAA_LORA, 0.5),
        'rwkv_g1': dense((nw, D, RWKV_GATE_LORA), D),
        'rwkv_g2': dense((nw, RWKV_GATE_LORA, D), RWKV_GATE_LORA),
        'rwkv_k_k': 0.85 + 0.05 * nrm((nw, D)),
        'rwkv_k_a': 1.0 + 0.05 * nrm((nw, D)),
        'rwkv_r_k': 0.1 * nrm((nw, RWKV_HEADS, RWKV_HEAD_DIM)),
        'rwkv_gn_gain': 1.0 + 0.02 * nrm((nw, D)),
        'rwkv_gn_bias': 0.02 * nrm((nw, D)),
        'rwkv_w_out': dense((nw, D, D), D, DEEPNORM_BETA),
        'ffn_w_up': dense((DEPTH, D, 2 * F), D),
        'ffn_conv_w': 0.5 * nrm((DEPTH, CONV_WIDTH, 2 * F)),
        'ffn_conv_b': 0.02 * nrm((DEPTH, 2 * F)),
        'ffn_w_down': dense((DEPTH, F, D), F, DEEPNORM_BETA),
        'ple_w_proj': dense((DEPTH, PLE_DIM, D), PLE_DIM, 0.5),
        'ple_w_gate': dense((DEPTH, D, D), D),
    }


def reference(x, p, ln_gain, ln_bias, ret_w_in, ret_gn_gain, ret_w_out,
              swa_w_qkv, swa_sinks, swa_w_out, rel_bias,
              rwkv_mix, rwkv_w_rkv, rwkv_w0, rwkv_w1, rwkv_w2, rwkv_a0, rwkv_a1, rwkv_a2,
              rwkv_g1, rwkv_g2, rwkv_k_k, rwkv_k_a, rwkv_r_k, rwkv_gn_gain, rwkv_gn_bias, rwkv_w_out,
              ffn_w_up, ffn_conv_w, ffn_conv_b, ffn_w_down, ple_w_proj, ple_w_gate):
    for i in range(DEPTH):
        kind, j = i % N_MIXERS, i // N_MIXERS
        if kind == 0:
            mixed = _retention(x, ret_w_in[j], ret_gn_gain[j], ret_w_out[j])
        elif kind == 1:
            mixed = _swa_attention(x, swa_w_qkv[j], swa_sinks[j], swa_w_out[j], rel_bias)
        else:
            mixed = _rwkv7_time_mix(x, rwkv_mix[j], rwkv_w_rkv[j], rwkv_w0[j], rwkv_w1[j], rwkv_w2[j],
                                    rwkv_a0[j], rwkv_a1[j], rwkv_a2[j], rwkv_g1[j], rwkv_g2[j],
                                    rwkv_k_k[j], rwkv_k_a[j], rwkv_r_k[j], rwkv_gn_gain[j],
                                    rwkv_gn_bias[j], rwkv_w_out[j])
        x = _layer_norm(DEEPNORM_ALPHA * x + mixed, ln_gain[i, 0], ln_bias[i, 0])
        ffn = _conv_ffn(x, ffn_w_up[i], ffn_conv_w[i], ffn_conv_b[i], ffn_w_down[i])
        x = _layer_norm(DEEPNORM_ALPHA * x + ffn, ln_gain[i, 1], ln_bias[i, 1])
        x = x + (p[i] @ ple_w_proj[i]) * jax.nn.sigmoid(x @ ple_w_gate[i])
    return x
```

```python
import functools
import math

import numpy as np
import jax
import jax.numpy as jnp
from jax import lax
from jax.experimental import pallas as pl
from jax.experimental.pallas import tpu as pltpu

F32 = jnp.float32
BF16 = jnp.bfloat16

D_MODEL = 2048
DEPTH = 4
N_MIXERS = 3

RET_HEADS = 8
RET_QK_DIM = D_MODEL // RET_HEADS
RET_V_DIM = 2 * D_MODEL // RET_HEADS
RET_CHUNK = 128
RET_GN_EPS = 1e-5
ROPE_BASE = 10000.0

SWA_HEAD_DIM = 64
SWA_Q_HEADS = D_MODEL // SWA_HEAD_DIM
SWA_KV_HEADS = SWA_Q_HEADS // 8
SWA_GROUP = SWA_Q_HEADS // SWA_KV_HEADS
SWA_WINDOW = 128
REL_BUCKETS = 32
REL_MAX_DIST = SWA_WINDOW
NEG_INF = -1e30

RWKV_HEAD_DIM = 64
RWKV_HEADS = D_MODEL // RWKV_HEAD_DIM
RWKV_GN_EPS = 64e-5
RWKV_CHUNK = 64

D_FF = 5504
CONV_WIDTH = 3
PLE_DIM = 256

LN_EPS = 1e-5
DEEPNORM_ALPHA = (2.0 * DEPTH) ** 0.25

LANES = 128
SUBLANES = 8
BF16_ROWS = 16
VMEM_LIMIT_BYTES = 56 * 1024 * 1024
D_FF_PAD = 5632


def _params(*semantics):
    return pltpu.CompilerParams(dimension_semantics=semantics,
                                vmem_limit_bytes=VMEM_LIMIT_BYTES)


def _dot(a, b):
    return jnp.dot(a, b, preferred_element_type=F32)


def _dot_nt(a, b):
    return lax.dot_general(a, b, (((1,), (1,)), ((), ())), preferred_element_type=F32)


def _dot_tn(a, b):
    return lax.dot_general(a, b, (((0,), (0,)), ((), ())), preferred_element_type=F32)


def _layer_norm_rows(z, gain, bias):
    mu = jnp.mean(z, axis=-1, keepdims=True)
    zc = z - mu
    var = jnp.mean(zc * zc, axis=-1, keepdims=True)
    return zc * lax.rsqrt(var + LN_EPS) * gain + bias


def _sigmoid(x):
    return 1.0 / (1.0 + jnp.exp(-x))


def _silu(x):
    return x * _sigmoid(x)


def _matmul_kernel(a_ref, w_ref, o_ref):
    o_ref[...] = _dot(a_ref[...], w_ref[...]).astype(o_ref.dtype)


def _matmul(a, w, *, tm, tn, out_dtype):
    m, k = a.shape
    n = w.shape[1]
    return pl.pallas_call(
        _matmul_kernel,
        grid=(m // tm, n // tn),
        in_specs=[pl.BlockSpec((tm, k), lambda i, j: (i, 0)),
                  pl.BlockSpec((k, tn), lambda i, j: (0, j))],
        out_specs=pl.BlockSpec((tm, tn), lambda i, j: (i, j)),
        out_shape=jax.ShapeDtypeStruct((m, n), out_dtype),
        compiler_params=_params("parallel", "arbitrary"),
    )(a, w)


def _matmul_ln_kernel(a_ref, w_ref, res_ref, gain_ref, bias_ref, xo_ref, xb_ref, *scratch, nk):
    part = _dot(a_ref[...], w_ref[...])

    def finalize(y):
        xn = _layer_norm_rows(DEEPNORM_ALPHA * res_ref[...] + y, gain_ref[...], bias_ref[...])
        xo_ref[...] = xn
        xb_ref[...] = xn.astype(BF16)

    if nk == 1:
        finalize(part)
        return
    acc_ref, = scratch
    k = pl.program_id(1)

    @pl.when(k == 0)
    def _():
        acc_ref[...] = part

    @pl.when((k > 0) & (k < nk - 1))
    def _():
        acc_ref[...] += part

    @pl.when(k == nk - 1)
    def _():
        finalize(acc_ref[...] + part)


def _matmul_ln(a, w, res, gain, bias, *, tm=512, tk=2048):
    m, kdim = a.shape
    d = w.shape[1]
    nk = kdim // tk
    row = lambda i, k: (i, 0)
    return pl.pallas_call(
        functools.partial(_matmul_ln_kernel, nk=nk),
        grid=(m // tm, nk),
        in_specs=[pl.BlockSpec((tm, tk), lambda i, k: (i, k)),
                  pl.BlockSpec((tk, d), lambda i, k: (k, 0)),
                  pl.BlockSpec((tm, d), row),
                  pl.BlockSpec((1, d), lambda i, k: (0, 0)),
                  pl.BlockSpec((1, d), lambda i, k: (0, 0))],
        out_specs=[pl.BlockSpec((tm, d), row), pl.BlockSpec((tm, d), row)],
        out_shape=[jax.ShapeDtypeStruct((m, d), F32), jax.ShapeDtypeStruct((m, d), BF16)],
        scratch_shapes=[pltpu.VMEM((tm, d), F32)] if nk > 1 else [],
        compiler_params=_params("parallel", "arbitrary"),
    )(a, w, res, gain, bias)


def _ffn_kernel(xb_ref, halo_ref, res_ref, wu_ref, wg_ref, cu_ref, cg_ref, wd_ref,
                gain_ref, bias_ref, xo_ref, xob_ref, acc_ref, *, tm, tiles_per_seq, nf):
    i = pl.program_id(0)
    f = pl.program_id(1)
    x = xb_ref[...]
    halo = jnp.where(i % tiles_per_seq == 0, jnp.zeros_like(halo_ref[...]), halo_ref[...])

    def conv_branch(w_ref, c_ref):
        h = jnp.concatenate([_dot(halo, w_ref[...]), _dot(x, w_ref[...])], axis=0)
        c = c_ref[...]
        out = c[3:4] + h[BF16_ROWS:] * c[2:3]
        out = out + pltpu.roll(h, 1, axis=0)[BF16_ROWS:] * c[1:2]
        out = out + pltpu.roll(h, 2, axis=0)[BF16_ROWS:] * c[0:1]
        return out

    u = conv_branch(wu_ref, cu_ref)
    gate = conv_branch(wg_ref, cg_ref)
    act = (_silu(gate) * u).astype(BF16)
    part = _dot(act, wd_ref[...])

    @pl.when(f == 0)
    def _():
        acc_ref[...] = part

    @pl.when((f > 0) & (f < nf - 1))
    def _():
        acc_ref[...] += part

    @pl.when(f == nf - 1)
    def _():
        y = acc_ref[...] + part
        xn = _layer_norm_rows(DEEPNORM_ALPHA * res_ref[...] + y, gain_ref[...], bias_ref[...])
        xo_ref[...] = xn
        xob_ref[...] = xn.astype(BF16)


def _ffn(xb, xres, wu, wg, cu, cg, wd, gain, bias, *, seq, tm=512, tf=512):
    m, d = xb.shape
    fp = wu.shape[1]
    nf = fp // tf
    halo_blocks = tm // BF16_ROWS
    row = lambda i, f: (i, 0)
    const = lambda i, f: (0, 0)
    return pl.pallas_call(
        functools.partial(_ffn_kernel, tm=tm, tiles_per_seq=seq // tm, nf=nf),
        grid=(m // tm, nf),
        in_specs=[pl.BlockSpec((tm, d), row),
                  pl.BlockSpec((BF16_ROWS, d), lambda i, f: (jnp.maximum(i * halo_blocks - 1, 0), 0)),
                  pl.BlockSpec((tm, d), row),
                  pl.BlockSpec((d, tf), lambda i, f: (0, f)),
                  pl.BlockSpec((d, tf), lambda i, f: (0, f)),
                  pl.BlockSpec((SUBLANES, tf), lambda i, f: (0, f)),
                  pl.BlockSpec((SUBLANES, tf), lambda i, f: (0, f)),
                  pl.BlockSpec((tf, d), lambda i, f: (f, 0)),
                  pl.BlockSpec((1, d), const),
                  pl.BlockSpec((1, d), const)],
        out_specs=[pl.BlockSpec((tm, d), row), pl.BlockSpec((tm, d), row)],
        out_shape=[jax.ShapeDtypeStruct((m, d), F32), jax.ShapeDtypeStruct((m, d), BF16)],
        scratch_shapes=[pltpu.VMEM((tm, d), F32)],
        compiler_params=_params("parallel", "arbitrary"),
    )(xb, xb, xres, wu, wg, cu, cg, wd, gain, bias)


def _ple_kernel(xb_ref, wg_ref, p_ref, wp_ref, res_ref, xo_ref, xob_ref):
    gate = _sigmoid(_dot(xb_ref[...], wg_ref[...]))
    proj = _dot(p_ref[...].astype(BF16), wp_ref[...])
    xn = res_ref[...] + proj * gate
    xo_ref[...] = xn
    xob_ref[...] = xn.astype(BF16)


def _ple(xb, xres, p, wg, wp, *, tm=1024, tn=1024):
    m, d = xb.shape
    pd = p.shape[1]
    return pl.pallas_call(
        _ple_kernel,
        grid=(m // tm, d // tn),
        in_specs=[pl.BlockSpec((tm, d), lambda i, j: (i, 0)),
                  pl.BlockSpec((d, tn), lambda i, j: (0, j)),
                  pl.BlockSpec((tm, pd), lambda i, j: (i, 0)),
                  pl.BlockSpec((pd, tn), lambda i, j: (0, j)),
                  pl.BlockSpec((tm, tn), lambda i, j: (i, j))],
        out_specs=[pl.BlockSpec((tm, tn), lambda i, j: (i, j)),
                   pl.BlockSpec((tm, tn), lambda i, j: (i, j))],
        out_shape=[jax.ShapeDtypeStruct((m, d), F32), jax.ShapeDtypeStruct((m, d), BF16)],
        compiler_params=_params("parallel", "arbitrary"),
    )(xb, wg, p, wp, xres)


def _ret_proj_kernel(a_ref, w_ref, cos_ref, sin_ref, o_ref, *, tn, nq, nkv):
    j = pl.program_id(1)
    acc = _dot(a_ref[...], w_ref[...])
    half = RET_QK_DIM // 2

    def rotary(scale):
        cos = cos_ref[...]
        sin = sin_ref[...]
        for h in range(tn // RET_QK_DIM):
            lo = h * RET_QK_DIM
            x1 = acc[:, lo:lo + half]
            x2 = acc[:, lo + half:lo + RET_QK_DIM]
            o_ref[:, lo:lo + half] = ((x1 * cos - x2 * sin) * scale).astype(o_ref.dtype)
            o_ref[:, lo + half:lo + RET_QK_DIM] = ((x1 * sin + x2 * cos) * scale).astype(o_ref.dtype)

    @pl.when(j < nq)
    def _():
        rotary(1.0)

    @pl.when((j >= nq) & (j < 2 * nq))
    def _():
        rotary(RET_QK_DIM ** -0.5)

    @pl.when((j >= 2 * nq) & (j < 2 * nq + nkv))
    def _():
        o_ref[...] = acc.astype(o_ref.dtype)

    @pl.when(j >= 2 * nq + nkv)
    def _():
        o_ref[...] = _silu(acc).astype(o_ref.dtype)


def _ret_proj(xb, w, cos, sin, *, seq, tm=1024, tn=1024):
    m, d = xb.shape
    n = w.shape[1]
    nq = RET_HEADS * RET_QK_DIM // tn
    nkv = RET_HEADS * RET_V_DIM // tn
    tiles_per_seq = seq // tm
    return pl.pallas_call(
        functools.partial(_ret_proj_kernel, tn=tn, nq=nq, nkv=nkv),
        grid=(m // tm, n // tn),
        in_specs=[pl.BlockSpec((tm, d), lambda i, j: (i, 0)),
                  pl.BlockSpec((d, tn), lambda i, j: (0, j)),
                  pl.BlockSpec((tm, RET_QK_DIM // 2), lambda i, j: (i % tiles_per_seq, 0)),
                  pl.BlockSpec((tm, RET_QK_DIM // 2), lambda i, j: (i % tiles_per_seq, 0))],
        out_specs=pl.BlockSpec((tm, tn), lambda i, j: (i, j)),
        out_shape=jax.ShapeDtypeStruct((m, n), BF16),
        compiler_params=_params("parallel", "arbitrary"),
    )(xb, w, cos, sin)


def _ret_core_kernel(q_ref, k_ref, v_ref, g_ref, lg_ref, gain_ref, o_ref, state_ref, *, ts):
    c = RET_CHUNK

    @pl.when(pl.program_id(2) == 0)
    def _():
        state_ref[...] = jnp.zeros_like(state_ref)

    lg = lg_ref[0]
    ii = lax.broadcasted_iota(jnp.int32, (c, c), 0)
    jj = lax.broadcasted_iota(jnp.int32, (c, c), 1)
    diff = (ii - jj).astype(F32)
    decay_mask = jnp.where(ii >= jj, jnp.exp(lg[:, :c] * jnp.maximum(diff, 0.0)), 0.0)
    row_v = lax.broadcasted_iota(jnp.int32, (c, RET_V_DIM), 0).astype(F32)
    row_k = lax.broadcasted_iota(jnp.int32, (c, RET_QK_DIM), 0).astype(F32)
    q_decay = jnp.exp(lg * (row_v + 1.0))
    k_decay = jnp.exp(lg[:, :RET_QK_DIM] * (c - 1.0 - row_k))
    chunk_decay = jnp.exp(lg * float(c))
    gain = gain_ref[...]

    for n in range(ts // c):
        rows = slice(n * c, (n + 1) * c)
        q = q_ref[rows, :]
        k = k_ref[rows, :]
        v = v_ref[rows, :]
        state = state_ref[...]
        scores = _dot_nt(q, k) * decay_mask
        inner = _dot(scores.astype(BF16), v)
        cross = _dot(q, state.astype(BF16)) * q_decay
        kd = (k.astype(F32) * k_decay).astype(BF16)
        state_ref[...] = state * chunk_decay + _dot_tn(kd, v)
        o = inner + cross
        mu = jnp.mean(o, axis=-1, keepdims=True)
        oc = o - mu
        var = jnp.mean(oc * oc, axis=-1, keepdims=True)
        on = oc * lax.rsqrt(var + RET_GN_EPS) * gain
        o_ref[rows, :] = (g_ref[rows, :].astype(F32) * on).astype(o_ref.dtype)


def _ret_core(proj, lg, gn_gain, *, batch, seq, ts=1024):
    m = proj.shape[0]
    nt = seq // ts
    kq = RET_HEADS
    row = lambda b, h, t: b * nt + t
    return pl.pallas_call(
        functools.partial(_ret_core_kernel, ts=ts),
        grid=(batch, RET_HEADS, nt),
        in_specs=[pl.BlockSpec((ts, RET_QK_DIM), lambda b, h, t: (row(b, h, t), h)),
                  pl.BlockSpec((ts, RET_QK_DIM), lambda b, h, t: (row(b, h, t), kq + h)),
                  pl.BlockSpec((ts, RET_V_DIM), lambda b, h, t: (row(b, h, t), kq + h)),
                  pl.BlockSpec((ts, RET_V_DIM), lambda b, h, t: (row(b, h, t), 2 * kq + h)),
                  pl.BlockSpec((1, 1, RET_V_DIM), lambda b, h, t: (h, 0, 0)),
                  pl.BlockSpec((1, RET_V_DIM), lambda b, h, t: (0, h))],
        out_specs=pl.BlockSpec((ts, RET_V_DIM), lambda b, h, t: (row(b, h, t), h)),
        out_shape=jax.ShapeDtypeStruct((m, RET_HEADS * RET_V_DIM), BF16),
        scratch_shapes=[pltpu.VMEM((RET_QK_DIM, RET_V_DIM), F32)],
        compiler_params=_params("parallel", "parallel", "arbitrary"),
    )(proj, proj, proj, proj, lg, gn_gain)


def _t5_buckets(dist):
    n = np.maximum(dist, 0)
    max_exact = REL_BUCKETS // 2
    large = max_exact + (np.log(np.maximum(n, 1) / max_exact) / np.log(REL_MAX_DIST / max_exact)
                         * (REL_BUCKETS - max_exact)).astype(np.int32)
    large = np.minimum(large, REL_BUCKETS - 1)
    return np.where(n < max_exact, n, large).astype(np.int32)


def _bias_table_kernel(rel_ref, bucket_ref, o_ref):
    h = pl.program_id(0)
    buckets = bucket_ref[...]
    table = jnp.zeros(buckets.shape, F32)
    for b in range(REL_BUCKETS):
        table = jnp.where(buckets == b, rel_ref[b, h], table)
    o_ref[0] = table


def _bias_table(rel_bias):
    c = SWA_WINDOW
    dist = np.arange(c)[:, None] + c - np.arange(2 * c)[None, :]
    buckets = jnp.asarray(_t5_buckets(dist))
    return pl.pallas_call(
        _bias_table_kernel,
        grid=(SWA_Q_HEADS,),
        in_specs=[pl.BlockSpec(memory_space=pltpu.SMEM),
                  pl.BlockSpec((c, 2 * c), lambda h: (0, 0))],
        out_specs=pl.BlockSpec((1, c, 2 * c), lambda h: (h, 0, 0)),
        out_shape=jax.ShapeDtypeStruct((SWA_Q_HEADS, c, 2 * c), F32),
        compiler_params=_params("arbitrary"),
    )(rel_bias, buckets)


def _swa_kernel(q_ref, kp_ref, kc_ref, vp_ref, vc_ref, bias_ref, sink_ref, o_ref, *, blocks_per_seq):
    c = SWA_WINDOW
    hd = SWA_HEAD_DIM
    has_prev = (pl.program_id(0) % blocks_per_seq) > 0
    ii = lax.broadcasted_iota(jnp.int32, (c, c), 0)
    jj = lax.broadcasted_iota(jnp.int32, (c, c), 1)
    prev_ok = (jj > ii) & has_prev
    cur_ok = jj <= ii
    scale = hd ** -0.5
    for pair in range(SWA_Q_HEADS // 2):
        outs = []
        for h in (2 * pair, 2 * pair + 1):
            kv = h // SWA_GROUP
            q = q_ref[:, h * hd:(h + 1) * hd]
            kp = kp_ref[:, kv * hd:(kv + 1) * hd]
            kc = kc_ref[:, kv * hd:(kv + 1) * hd]
            vp = vp_ref[:, kv * hd:(kv + 1) * hd]
            vc = vc_ref[:, kv * hd:(kv + 1) * hd]
            sp = jnp.where(prev_ok, _dot_nt(q, kp) * scale + bias_ref[h, :, :c], NEG_INF)
            sc = jnp.where(cur_ok, _dot_nt(q, kc) * scale + bias_ref[h, :, c:], NEG_INF)
            sink = sink_ref[h]
            mx = jnp.maximum(jnp.maximum(jnp.max(sp, axis=-1, keepdims=True),
                                         jnp.max(sc, axis=-1, keepdims=True)), sink)
            pp = jnp.exp(sp - mx)
            pc = jnp.exp(sc - mx)
            denom = (jnp.sum(pp, axis=-1, keepdims=True) + jnp.sum(pc, axis=-1, keepdims=True)
                     + jnp.exp(sink - mx))
            inv = 1.0 / denom
            o = _dot((pp * inv).astype(BF16), vp) + _dot((pc * inv).astype(BF16), vc)
            outs.append(o)
        o_ref[:, pair * 2 * hd:(pair + 1) * 2 * hd] = jnp.concatenate(outs, axis=-1).astype(o_ref.dtype)


def _swa_core(qkv, bias, sinks, *, seq):
    m = qkv.shape[0]
    c = SWA_WINDOW
    blocks_per_seq = seq // c
    qw = SWA_Q_HEADS * SWA_HEAD_DIM
    kw = SWA_KV_HEADS * SWA_HEAD_DIM
    kcol = qw // kw
    prev = lambda r: jnp.where(r % blocks_per_seq == 0, r, r - 1)
    return pl.pallas_call(
        functools.partial(_swa_kernel, blocks_per_seq=blocks_per_seq),
        grid=(m // c,),
        in_specs=[pl.BlockSpec((c, qw), lambda r: (r, 0)),
                  pl.BlockSpec((c, kw), lambda r: (prev(r), kcol)),
                  pl.BlockSpec((c, kw), lambda r: (r, kcol)),
                  pl.BlockSpec((c, kw), lambda r: (prev(r), kcol + 1)),
                  pl.BlockSpec((c, kw), lambda r: (r, kcol + 1)),
                  pl.BlockSpec((SWA_Q_HEADS, c, 2 * c), lambda r: (0, 0, 0)),
                  pl.BlockSpec(memory_space=pltpu.SMEM)],
        out_specs=pl.BlockSpec((c, qw), lambda r: (r, 0)),
        out_shape=jax.ShapeDtypeStruct((m, qw), BF16),
        compiler_params=_params("parallel"),
    )(qkv, qkv, qkv, qkv, qkv, bias, sinks)


def _shifted_delta(x, halo_ref, at_seq_start):
    first = jnp.where(at_seq_start, jnp.zeros_like(halo_ref[SUBLANES - 1:, :]), halo_ref[SUBLANES - 1:, :])
    rows = lax.broadcasted_iota(jnp.int32, x.shape, 0)
    prev = jnp.where(rows == 0, first, pltpu.roll(x, 1, axis=0))
    return prev - x


def _rwkv_rkv_kernel(x_ref, halo_ref, mix_ref, w_ref, o_ref, amix_ref, *, tiles_per_seq, nj):
    i = pl.program_id(0)
    j = pl.program_id(1)

    @pl.when(j % nj == 0)
    def _():
        x = x_ref[...]
        xx = _shifted_delta(x, halo_ref, i % tiles_per_seq == 0)
        amix_ref[...] = (x + xx * mix_ref[0]).astype(BF16)

    o_ref[...] = _dot(amix_ref[...], w_ref[...])


def _rwkv_rkv(x, mix3, w3, *, seq, tm=512, tn=1024):
    m, d = x.shape
    nj = d // tn
    halo_blocks = tm // SUBLANES
    return pl.pallas_call(
        functools.partial(_rwkv_rkv_kernel, tiles_per_seq=seq // tm, nj=nj),
        grid=(m // tm, 3 * nj),
        in_specs=[pl.BlockSpec((tm, d), lambda i, j: (i, 0)),
                  pl.BlockSpec((SUBLANES, d), lambda i, j: (jnp.maximum(i * halo_blocks - 1, 0), 0)),
                  pl.BlockSpec((1, 1, d), lambda i, j: (j // nj, 0, 0)),
                  pl.BlockSpec((None, d, tn), lambda i, j: (j // nj, 0, j % nj))],
        out_specs=pl.BlockSpec((None, tm, tn), lambda i, j: (j // nj, i, j % nj)),
        out_shape=jax.ShapeDtypeStruct((3, m, d), F32),
        scratch_shapes=[pltpu.VMEM((tm, d), BF16)],
        compiler_params=_params("parallel", "arbitrary"),
    )(x, x, mix3, w3)


def _rwkv_lora_kernel(x_ref, halo_ref, mix_ref, w0_ref, a0_ref, w1_ref, w2_ref, a1_ref, a2_ref,
                      g1_ref, g2_ref, ld_ref, a_ref, g_ref, *, tiles_per_seq):
    i = pl.program_id(0)
    x = x_ref[...]
    xx = _shifted_delta(x, halo_ref, i % tiles_per_seq == 0)
    mix = mix_ref[...]
    xw = (x + xx * mix[0:1]).astype(BF16)
    xa = (x + xx * mix[1:2]).astype(BF16)
    xg = (x + xx * mix[2:3]).astype(BF16)
    z = w0_ref[...] + _dot(jnp.tanh(_dot(xw, w1_ref[...])).astype(BF16), w2_ref[...])
    ld_ref[...] = -math.exp(-0.5) * _sigmoid(z)
    a_ref[...] = _sigmoid(a0_ref[...] + _dot(_dot(xa, a1_ref[...]).astype(BF16), a2_ref[...]))
    g_ref[...] = _dot(_sigmoid(_dot(xg, g1_ref[...])).astype(BF16), g2_ref[...]).astype(g_ref.dtype)


def _rwkv_lora(x, mix3, w0, a0, w1, w2, a1, a2, g1, g2, *, seq, tm=512):
    m, d = x.shape
    halo_blocks = tm // SUBLANES
    row = lambda i: (i, 0)
    full = lambda a: pl.BlockSpec(a.shape, lambda i: (0, 0))
    return pl.pallas_call(
        functools.partial(_rwkv_lora_kernel, tiles_per_seq=seq // tm),
        grid=(m // tm,),
        in_specs=[pl.BlockSpec((tm, d), row),
                  pl.BlockSpec((SUBLANES, d), lambda i: (jnp.maximum(i * halo_blocks - 1, 0), 0)),
                  full(mix3), full(w0), full(a0), full(w1), full(w2), full(a1), full(a2),
                  full(g1), full(g2)],
        out_specs=[pl.BlockSpec((tm, d), row)] * 3,
        out_shape=[jax.ShapeDtypeStruct((m, d), F32), jax.ShapeDtypeStruct((m, d), F32),
                   jax.ShapeDtypeStruct((m, d), BF16)],
        compiler_params=_params("parallel"),
    )(x, x, mix3, w0, a0, w1, w2, a1, a2, g1, g2)


def _cumsum_rows(x):
    n = x.shape[0]
    rows = lax.broadcasted_iota(jnp.int32, x.shape, 0)
    shift = 1
    while shift < n:
        x = x + jnp.where(rows >= shift, pltpu.roll(x, shift, axis=0), 0.0)
        shift *= 2
    return x


def _rwkv_chunk(st, r, ld, k, v, a, kk_w, ka_w, rk_w, gain, bias):
    n = r.shape[0]
    bf = lambda t: t.astype(BF16)
    kk = k * kk_w
    kk = kk / jnp.maximum(jnp.sqrt(jnp.sum(kk * kk, axis=-1, keepdims=True)), 1e-12)
    km = k * (1.0 + (a - 1.0) * ka_w)
    av = -kk
    bv = kk * a

    cw = _cumsum_rows(ld)
    w_in = jnp.exp(cw)
    w_inv = jnp.exp(-cw)
    a_t = av * jnp.exp(cw - ld)
    r_t = r * w_in
    b_t = bv * w_inv
    k_t = km * w_inv
    cw_last = cw[n - 1:n, :]
    to_end = jnp.exp(cw_last - cw)
    b_e = bv * to_end
    k_e = km * to_end

    ii = lax.broadcasted_iota(jnp.int32, (n, n), 0)
    jj = lax.broadcasted_iota(jnp.int32, (n, n), 1)
    strict = ii > jj
    incl = ii >= jj
    ar = bf(jnp.concatenate([a_t, r_t], axis=0))
    xb = _dot_nt(ar, bf(b_t))
    xk = _dot_nt(ar, bf(k_t))
    a_ab = jnp.where(strict, xb[:n], 0.0)
    a_ak = jnp.where(strict, xk[:n], 0.0)
    a_rb = jnp.where(incl, xb[n:], 0.0)
    a_rk = jnp.where(incl, xk[n:], 0.0)

    inv = jnp.where(ii == jj, 1.0, a_ab)
    power = a_ab
    for _ in range(int(math.log2(n)) - 1):
        power = _dot(bf(power), bf(power))
        inv = inv + _dot(bf(inv), bf(power))

    st_b = bf(st)
    v_b = bf(v)
    z = _dot(bf(a_t), st_b) + _dot(bf(a_ak), v_b)
    u_b = bf(_dot(bf(inv), bf(z)))
    y = _dot(bf(r_t), st_b) + _dot(bf(a_rb), u_b) + _dot(bf(a_rk), v_b)

    hd = st.shape[0]
    di = lax.broadcasted_iota(jnp.int32, (hd, hd), 0)
    dj = lax.broadcasted_iota(jnp.int32, (hd, hd), 1)
    w_col = jnp.sum(jnp.where(di == dj, jnp.exp(cw_last), 0.0), axis=-1, keepdims=True)
    st_new = st * w_col + _dot_tn(bf(b_e), u_b) + _dot_tn(bf(k_e), v_b)

    mu = jnp.mean(y, axis=-1, keepdims=True)
    yc = y - mu
    var = jnp.mean(yc * yc, axis=-1, keepdims=True)
    yn = yc * lax.rsqrt(var + RWKV_GN_EPS) * gain + bias
    bonus = jnp.sum(r * km * rk_w, axis=-1, keepdims=True) * v
    return st_new, yn + bonus


def _rwkv_core_kernel(r_ref, k_ref, v_ref, ld_ref, a_ref, g_ref, par_ref, o_ref, state_ref, *, tc):
    hd = RWKV_HEAD_DIM
    n = RWKV_CHUNK

    @pl.when(pl.program_id(2) == 0)
    def _():
        state_ref[...] = jnp.zeros_like(state_ref)

    par = par_ref[...]
    for c in range(tc // n):
        rows = slice(c * n, (c + 1) * n)
        outs = []
        for h in range(LANES // hd):
            cols = slice(h * hd, (h + 1) * hd)
            st, y = _rwkv_chunk(
                state_ref[h], r_ref[rows, cols], ld_ref[rows, cols], k_ref[rows, cols],
                v_ref[rows, cols], a_ref[rows, cols],
                par[0:1, cols], par[1:2, cols], par[2:3, cols], par[3:4, cols], par[4:5, cols])
            state_ref[h] = st
            outs.append(y)
        y2 = jnp.concatenate(outs, axis=-1)
        o_ref[rows, :] = (y2.astype(F32) * g_ref[rows, :].astype(F32)).astype(o_ref.dtype)


def _rwkv_core(rkv, ld, a, g, par, *, batch, seq, tc=512):
    m, d = ld.shape
    nt = seq // tc
    blk = lambda b, hp, t: (b * nt + t, hp)
    sel = lambda which: pl.BlockSpec((None, tc, LANES), lambda b, hp, t: (which, b * nt + t, hp))
    return pl.pallas_call(
        functools.partial(_rwkv_core_kernel, tc=tc),
        grid=(batch, d // LANES, nt),
        in_specs=[sel(0), sel(1), sel(2),
                  pl.BlockSpec((tc, LANES), blk),
                  pl.BlockSpec((tc, LANES), blk),
                  pl.BlockSpec((tc, LANES), blk),
                  pl.BlockSpec((SUBLANES, LANES), lambda b, hp, t: (0, hp))],
        out_specs=pl.BlockSpec((tc, LANES), blk),
        out_shape=jax.ShapeDtypeStruct((m, d), BF16),
        scratch_shapes=[pltpu.VMEM((LANES // RWKV_HEAD_DIM, RWKV_HEAD_DIM, RWKV_HEAD_DIM), F32)],
        compiler_params=_params("parallel", "parallel", "arbitrary"),
    )(rkv, rkv, rkv, ld, a, g, par)


def _pad_cols(w, n):
    return jnp.pad(w, ((0, 0), (0, n - w.shape[1])))


def _pad_rows(w, n):
    return jnp.pad(w, ((0, n - w.shape[0]), (0, 0)))


def kernel(x, p, ln_gain, ln_bias, ret_w_in, ret_gn_gain, ret_w_out, swa_w_qkv, swa_sinks, swa_w_out, rel_bias, rwkv_mix, rwkv_w_rkv, rwkv_w0, rwkv_w1, rwkv_w2, rwkv_a0, rwkv_a1, rwkv_a2, rwkv_g1, rwkv_g2, rwkv_k_k, rwkv_k_a, rwkv_r_k, rwkv_gn_gain, rwkv_gn_bias, rwkv_w_out, ffn_w_up, ffn_conv_w, ffn_conv_b, ffn_w_down, ple_w_proj, ple_w_gate):
    batch, seq, d = x.shape
    m = batch * seq
    xf = x.reshape(m, d)
    xb = xf.astype(BF16)
    row = lambda v: v.reshape(1, -1)

    half = RET_QK_DIM // 2
    inv_freq = 1.0 / (ROPE_BASE ** (jnp.arange(0, RET_QK_DIM, 2, dtype=F32) / RET_QK_DIM))
    ang = jnp.arange(seq, dtype=F32)[:, None] * inv_freq[None, :]
    cos, sin = jnp.cos(ang), jnp.sin(ang)
    log_gamma = jnp.log(1.0 - 2.0 ** (-5.0 - jnp.arange(RET_HEADS, dtype=F32)))
    lg = jnp.broadcast_to(log_gamma[:, None, None], (RET_HEADS, 1, RET_V_DIM))

    for i in range(DEPTH):
        kind, j = i % N_MIXERS, i // N_MIXERS
        g1, b1 = row(ln_gain[i, 0]), row(ln_bias[i, 0])
        if kind == 0:
            proj = _ret_proj(xb, ret_w_in[j].astype(BF16), cos, sin, seq=seq)
            o = _ret_core(proj, lg, row(ret_gn_gain[j]), batch=batch, seq=seq)
            xf, xb = _matmul_ln(o, ret_w_out[j].astype(BF16), xf, g1, b1)
        elif kind == 1:
            qkv = _matmul(xb, swa_w_qkv[j].astype(BF16), tm=1024, tn=1280, out_dtype=BF16)
            o = _swa_core(qkv, _bias_table(rel_bias), swa_sinks[j], seq=seq)
            xf, xb = _matmul_ln(o, swa_w_out[j].astype(BF16), xf, g1, b1)
        else:
            mix = rwkv_mix[j]
            rkv = _rwkv_rkv(xf, mix[jnp.array([0, 2, 3])].reshape(3, 1, d),
                            rwkv_w_rkv[j].astype(BF16), seq=seq)
            lora = RWKV_HEAD_DIM * 2
            ld, a, g = _rwkv_lora(
                xf, mix[jnp.array([1, 4, 5])], row(rwkv_w0[j]), row(rwkv_a0[j]),
                _pad_cols(rwkv_w1[j], lora).astype(BF16), _pad_rows(rwkv_w2[j], lora).astype(BF16),
                _pad_cols(rwkv_a1[j], lora).astype(BF16), _pad_rows(rwkv_a2[j], lora).astype(BF16),
                rwkv_g1[j].astype(BF16), rwkv_g2[j].astype(BF16), seq=seq)
            par = jnp.stack([rwkv_k_k[j], rwkv_k_a[j], rwkv_r_k[j].reshape(-1), rwkv_gn_gain[j],
                             rwkv_gn_bias[j]] + [jnp.zeros((d,), F32)] * 3)
            o = _rwkv_core(rkv, ld, a, g, par, batch=batch, seq=seq)
            xf, xb = _matmul_ln(o, rwkv_w_out[j].astype(BF16), xf, g1, b1)

        w_up = ffn_w_up[i]
        conv = jnp.concatenate([ffn_conv_w[i], ffn_conv_b[i][None]], axis=0)
        conv = jnp.pad(conv, ((0, SUBLANES - conv.shape[0]), (0, 0)))
        xf, xb = _ffn(xb, xf,
                      _pad_cols(w_up[:, :D_FF], D_FF_PAD).astype(BF16),
                      _pad_cols(w_up[:, D_FF:], D_FF_PAD).astype(BF16),
                      _pad_cols(conv[:, :D_FF], D_FF_PAD), _pad_cols(conv[:, D_FF:], D_FF_PAD),
                      _pad_rows(ffn_w_down[i], D_FF_PAD).astype(BF16),
                      row(ln_gain[i, 1]), row(ln_bias[i, 1]), seq=seq)
        xf, xb = _ple(xb, xf, p[i].reshape(m, PLE_DIM), ple_w_gate[i].astype(BF16),
                      ple_w_proj[i].astype(BF16))
    return xf.reshape(batch, seq, d)
```

```python
import functools
import math

import numpy as np
import jax
import jax.numpy as jnp
from jax import lax
from jax.experimental import pallas as pl
from jax.experimental.pallas import tpu as pltpu

F32 = jnp.float32
BF16 = jnp.bfloat16

D_MODEL = 2048
DEPTH = 4
N_MIXERS = 3

RET_HEADS = 8
RET_QK_DIM = D_MODEL // RET_HEADS
RET_V_DIM = 2 * D_MODEL // RET_HEADS
RET_CHUNK = 128
RET_GN_EPS = 1e-5
ROPE_BASE = 10000.0

SWA_HEAD_DIM = 64
SWA_Q_HEADS = D_MODEL // SWA_HEAD_DIM
SWA_KV_HEADS = SWA_Q_HEADS // 8
SWA_GROUP = SWA_Q_HEADS // SWA_KV_HEADS
SWA_WINDOW = 128
REL_BUCKETS = 32
REL_MAX_DIST = SWA_WINDOW
NEG_INF = -1e30

RWKV_HEAD_DIM = 64
RWKV_HEADS = D_MODEL // RWKV_HEAD_DIM
RWKV_GN_EPS = 64e-5
RWKV_CHUNK = 64

D_FF = 5504
CONV_WIDTH = 3
PLE_DIM = 256

LN_EPS = 1e-5
DEEPNORM_ALPHA = (2.0 * DEPTH) ** 0.25

LANES = 128
SUBLANES = 8
BF16_ROWS = 16
VMEM_LIMIT_BYTES = 56 * 1024 * 1024
FFN_COL_TILE = 512
D_FF_PAD = -(-D_FF // FFN_COL_TILE) * FFN_COL_TILE


def _params(*semantics, flags=None):
    return pltpu.CompilerParams(dimension_semantics=semantics,
                                vmem_limit_bytes=VMEM_LIMIT_BYTES, flags=flags)


def _dot(a, b):
    return jnp.dot(a, b, preferred_element_type=F32)


def _dot_nt(a, b):
    return lax.dot_general(a, b, (((1,), (1,)), ((), ())), preferred_element_type=F32)


def _dot_tn(a, b):
    return lax.dot_general(a, b, (((0,), (0,)), ((), ())), preferred_element_type=F32)


def _layer_norm_rows(z, gain, bias):
    mu = jnp.mean(z, axis=-1, keepdims=True)
    zc = z - mu
    var = jnp.mean(zc * zc, axis=-1, keepdims=True)
    return zc * lax.rsqrt(var + LN_EPS) * gain + bias


def _sigmoid(x):
    return 1.0 / (1.0 + jnp.exp(-x))


def _silu(x):
    return x * _sigmoid(x)


def _matmul_kernel(a_ref, w_ref, o_ref):
    o_ref[...] = _dot(a_ref[...], w_ref[...]).astype(o_ref.dtype)


def _matmul(a, w, *, tm, tn, out_dtype):
    m, k = a.shape
    n = w.shape[1]
    return pl.pallas_call(
        _matmul_kernel,
        grid=(m // tm, n // tn),
        in_specs=[pl.BlockSpec((tm, k), lambda i, j: (i, 0)),
                  pl.BlockSpec((k, tn), lambda i, j: (0, j))],
        out_specs=pl.BlockSpec((tm, tn), lambda i, j: (i, j)),
        out_shape=jax.ShapeDtypeStruct((m, n), out_dtype),
        compiler_params=_params("parallel", "arbitrary"),
    )(a, w)


def _matmul_ln_kernel(a_ref, w_ref, res_ref, gain_ref, bias_ref, xo_ref, xb_ref, *scratch, nk):
    part = _dot(a_ref[...], w_ref[...])

    def finalize(y):
        xn = _layer_norm_rows(DEEPNORM_ALPHA * res_ref[...] + y, gain_ref[...], bias_ref[...])
        xo_ref[...] = xn
        xb_ref[...] = xn.astype(BF16)

    if nk == 1:
        finalize(part)
        return
    acc_ref, = scratch
    k = pl.program_id(1)

    @pl.when(k == 0)
    def _():
        acc_ref[...] = part

    @pl.when((k > 0) & (k < nk - 1))
    def _():
        acc_ref[...] += part

    @pl.when(k == nk - 1)
    def _():
        finalize(acc_ref[...] + part)


def _matmul_ln(a, w, res, gain, bias, *, tm=512, tk=2048):
    m, kdim = a.shape
    d = w.shape[1]
    nk = kdim // tk
    row = lambda i, k: (i, 0)
    return pl.pallas_call(
        functools.partial(_matmul_ln_kernel, nk=nk),
        grid=(m // tm, nk),
        in_specs=[pl.BlockSpec((tm, tk), lambda i, k: (i, k)),
                  pl.BlockSpec((tk, d), lambda i, k: (k, 0)),
                  pl.BlockSpec((tm, d), row),
                  pl.BlockSpec((1, d), lambda i, k: (0, 0)),
                  pl.BlockSpec((1, d), lambda i, k: (0, 0))],
        out_specs=[pl.BlockSpec((tm, d), row), pl.BlockSpec((tm, d), row)],
        out_shape=[jax.ShapeDtypeStruct((m, d), F32), jax.ShapeDtypeStruct((m, d), BF16)],
        scratch_shapes=[pltpu.VMEM((tm, d), F32)] if nk > 1 else [],
        compiler_params=_params("parallel", "arbitrary"),
    )(a, w, res, gain, bias)


def _ffn_kernel(xb_ref, halo_ref, res_ref, wu_ref, wg_ref, cu_ref, cg_ref, wd_ref,
                gain_ref, bias_ref, xo_ref, xob_ref, acc_ref, *, tm, tiles_per_seq, nf, tf, sub, row_split):
    i = pl.program_id(0)
    f = pl.program_id(1)
    halo = jnp.where(i % tiles_per_seq == 0, jnp.zeros_like(halo_ref[...]), halo_ref[...])
    x = jnp.concatenate([halo, xb_ref[...]], axis=0)

    def conv_branch(xr, w_ref, c_ref, cols):
        h = _dot(xr, w_ref[:, cols])
        c = c_ref[:, cols]
        out = c[3:4] + h[BF16_ROWS:] * c[2:3]
        out = out + pltpu.roll(h, 1, axis=0)[BF16_ROWS:] * c[1:2]
        out = out + pltpu.roll(h, 2, axis=0)[BF16_ROWS:] * c[0:1]
        return out

    @pl.when(f == 0)
    def _():
        acc_ref[...] = jnp.zeros_like(acc_ref)

    rows = tm // row_split
    for rb in range(row_split):
        xr = x[rb * rows:(rb + 1) * rows + BF16_ROWS]
        acts = []
        for s in range(tf // sub):
            cols = slice(s * sub, (s + 1) * sub)
            u = conv_branch(xr, wu_ref, cu_ref, cols)
            gate = conv_branch(xr, wg_ref, cg_ref, cols)
            acts.append((_silu(gate) * u).astype(BF16))
        acc_ref[rb * rows:(rb + 1) * rows, :] += _dot(jnp.concatenate(acts, axis=1), wd_ref[...])

    @pl.when(f == nf - 1)
    def _():
        xn = _layer_norm_rows(DEEPNORM_ALPHA * res_ref[...] + acc_ref[...], gain_ref[...],
                              bias_ref[...])
        xo_ref[...] = xn
        xob_ref[...] = xn.astype(BF16)


def _ffn(xb, xres, wu, wg, cu, cg, wd, gain, bias, *, seq, tm=512, tf=FFN_COL_TILE, sub=256,
         row_split=2):
    m, d = xb.shape
    fp = wu.shape[1]
    nf = fp // tf
    halo_blocks = tm // BF16_ROWS
    row = lambda i, f: (i, 0)
    const = lambda i, f: (0, 0)
    once = pl.Buffered(1)
    return pl.pallas_call(
        functools.partial(_ffn_kernel, tm=tm, tiles_per_seq=seq // tm, nf=nf, tf=tf, sub=sub,
                          row_split=row_split),
        grid=(m // tm, nf),
        in_specs=[pl.BlockSpec((tm, d), row),
                  pl.BlockSpec((BF16_ROWS, d), lambda i, f: (jnp.maximum(i * halo_blocks - 1, 0), 0)),
                  pl.BlockSpec((tm, d), row, pipeline_mode=once),
                  pl.BlockSpec((d, tf), lambda i, f: (0, f)),
                  pl.BlockSpec((d, tf), lambda i, f: (0, f)),
                  pl.BlockSpec((SUBLANES, tf), lambda i, f: (0, f)),
                  pl.BlockSpec((SUBLANES, tf), lambda i, f: (0, f)),
                  pl.BlockSpec((tf, d), lambda i, f: (f, 0)),
                  pl.BlockSpec((1, d), const),
                  pl.BlockSpec((1, d), const)],
        out_specs=[pl.BlockSpec((tm, d), row, pipeline_mode=once),
                   pl.BlockSpec((tm, d), row, pipeline_mode=once)],
        out_shape=[jax.ShapeDtypeStruct((m, d), F32), jax.ShapeDtypeStruct((m, d), BF16)],
        scratch_shapes=[pltpu.VMEM((tm, d), F32)],
        compiler_params=_params("parallel", "arbitrary"),
    )(xb, xb, xres, wu, wg, cu, cg, wd, gain, bias)


def _ple_kernel(xb_ref, wg_ref, p_ref, wp_ref, res_ref, xo_ref, xob_ref):
    gate = _sigmoid(_dot(xb_ref[...], wg_ref[...]))
    proj = _dot(p_ref[...].astype(BF16), wp_ref[...])
    xn = res_ref[...] + proj * gate
    xo_ref[...] = xn
    xob_ref[...] = xn.astype(BF16)


def _ple(xb, xres, p, wg, wp, *, tm=1024, tn=1024):
    m, d = xb.shape
    pd = p.shape[1]
    return pl.pallas_call(
        _ple_kernel,
        grid=(m // tm, d // tn),
        in_specs=[pl.BlockSpec((tm, d), lambda i, j: (i, 0)),
                  pl.BlockSpec((d, tn), lambda i, j: (0, j)),
                  pl.BlockSpec((tm, pd), lambda i, j: (i, 0)),
                  pl.BlockSpec((pd, tn), lambda i, j: (0, j)),
                  pl.BlockSpec((tm, tn), lambda i, j: (i, j))],
        out_specs=[pl.BlockSpec((tm, tn), lambda i, j: (i, j)),
                   pl.BlockSpec((tm, tn), lambda i, j: (i, j))],
        out_shape=[jax.ShapeDtypeStruct((m, d), F32), jax.ShapeDtypeStruct((m, d), BF16)],
        compiler_params=_params("parallel", "arbitrary"),
    )(xb, wg, p, wp, xres)


def _ret_proj_kernel(a_ref, w_ref, cos_ref, sin_ref, o_ref, *, tn, nq, nkv):
    j = pl.program_id(1)
    acc = _dot(a_ref[...], w_ref[...])
    half = RET_QK_DIM // 2

    def rotary(scale):
        cos = cos_ref[...]
        sin = sin_ref[...]
        for h in range(tn // RET_QK_DIM):
            lo = h * RET_QK_DIM
            x1 = acc[:, lo:lo + half]
            x2 = acc[:, lo + half:lo + RET_QK_DIM]
            o_ref[:, lo:lo + half] = ((x1 * cos - x2 * sin) * scale).astype(o_ref.dtype)
            o_ref[:, lo + half:lo + RET_QK_DIM] = ((x1 * sin + x2 * cos) * scale).astype(o_ref.dtype)

    @pl.when(j < nq)
    def _():
        rotary(1.0)

    @pl.when((j >= nq) & (j < 2 * nq))
    def _():
        rotary(RET_QK_DIM ** -0.5)

    @pl.when((j >= 2 * nq) & (j < 2 * nq + nkv))
    def _():
        o_ref[...] = acc.astype(o_ref.dtype)

    @pl.when(j >= 2 * nq + nkv)
    def _():
        o_ref[...] = _silu(acc).astype(o_ref.dtype)


def _ret_proj(xb, w, cos, sin, *, seq, tm=1024, tn=1024):
    m, d = xb.shape
    n = w.shape[1]
    nq = RET_HEADS * RET_QK_DIM // tn
    nkv = RET_HEADS * RET_V_DIM // tn
    tiles_per_seq = seq // tm
    return pl.pallas_call(
        functools.partial(_ret_proj_kernel, tn=tn, nq=nq, nkv=nkv),
        grid=(m // tm, n // tn),
        in_specs=[pl.BlockSpec((tm, d), lambda i, j: (i, 0)),
                  pl.BlockSpec((d, tn), lambda i, j: (0, j)),
                  pl.BlockSpec((tm, RET_QK_DIM // 2), lambda i, j: (i % tiles_per_seq, 0)),
                  pl.BlockSpec((tm, RET_QK_DIM // 2), lambda i, j: (i % tiles_per_seq, 0))],
        out_specs=pl.BlockSpec((tm, tn), lambda i, j: (i, j)),
        out_shape=jax.ShapeDtypeStruct((m, n), BF16),
        compiler_params=_params("parallel", "arbitrary"),
    )(xb, w, cos, sin)


def _ret_core_kernel(q_ref, k_ref, v_ref, g_ref, lg_ref, gain_ref, o_ref, state_ref, *, ts):
    c = RET_CHUNK

    @pl.when(pl.program_id(2) == 0)
    def _():
        state_ref[...] = jnp.zeros_like(state_ref)

    lg = lg_ref[0]
    ii = lax.broadcasted_iota(jnp.int32, (c, c), 0)
    jj = lax.broadcasted_iota(jnp.int32, (c, c), 1)
    diff = (ii - jj).astype(F32)
    decay_mask = jnp.where(ii >= jj, jnp.exp(lg[:, :c] * jnp.maximum(diff, 0.0)), 0.0)
    row_v = lax.broadcasted_iota(jnp.int32, (c, RET_V_DIM), 0).astype(F32)
    row_k = lax.broadcasted_iota(jnp.int32, (c, RET_QK_DIM), 0).astype(F32)
    q_decay = jnp.exp(lg * (row_v + 1.0))
    k_decay = jnp.exp(lg[:, :RET_QK_DIM] * (c - 1.0 - row_k))
    chunk_decay = jnp.exp(lg * float(c))
    gain = gain_ref[...]

    for n in range(ts // c):
        rows = slice(n * c, (n + 1) * c)
        q = q_ref[rows, :]
        k = k_ref[rows, :]
        v = v_ref[rows, :]
        state = state_ref[...]
        scores = _dot_nt(q, k) * decay_mask
        inner = _dot(scores.astype(BF16), v)
        cross = _dot(q, state.astype(BF16)) * q_decay
        kd = (k.astype(F32) * k_decay).astype(BF16)
        state_ref[...] = state * chunk_decay + _dot_tn(kd, v)
        o = inner + cross
        mu = jnp.mean(o, axis=-1, keepdims=True)
        oc = o - mu
        var = jnp.mean(oc * oc, axis=-1, keepdims=True)
        on = oc * lax.rsqrt(var + RET_GN_EPS) * gain
        o_ref[rows, :] = (g_ref[rows, :].astype(F32) * on).astype(o_ref.dtype)


def _ret_core(proj, lg, gn_gain, *, batch, seq, ts=1024):
    m = proj.shape[0]
    nt = seq // ts
    kq = RET_HEADS
    row = lambda b, h, t: b * nt + t
    return pl.pallas_call(
        functools.partial(_ret_core_kernel, ts=ts),
        grid=(batch, RET_HEADS, nt),
        in_specs=[pl.BlockSpec((ts, RET_QK_DIM), lambda b, h, t: (row(b, h, t), h)),
                  pl.BlockSpec((ts, RET_QK_DIM), lambda b, h, t: (row(b, h, t), kq + h)),
                  pl.BlockSpec((ts, RET_V_DIM), lambda b, h, t: (row(b, h, t), kq + h)),
                  pl.BlockSpec((ts, RET_V_DIM), lambda b, h, t: (row(b, h, t), 2 * kq + h)),
                  pl.BlockSpec((1, 1, RET_V_DIM), lambda b, h, t: (h, 0, 0)),
                  pl.BlockSpec((1, RET_V_DIM), lambda b, h, t: (0, h))],
        out_specs=pl.BlockSpec((ts, RET_V_DIM), lambda b, h, t: (row(b, h, t), h)),
        out_shape=jax.ShapeDtypeStruct((m, RET_HEADS * RET_V_DIM), BF16),
        scratch_shapes=[pltpu.VMEM((RET_QK_DIM, RET_V_DIM), F32)],
        compiler_params=_params("parallel", "parallel", "arbitrary"),
    )(proj, proj, proj, proj, lg, gn_gain)


def _t5_buckets(dist):
    n = np.maximum(dist, 0)
    max_exact = REL_BUCKETS // 2
    large = max_exact + (np.log(np.maximum(n, 1) / max_exact) / np.log(REL_MAX_DIST / max_exact)
                         * (REL_BUCKETS - max_exact)).astype(np.int32)
    large = np.minimum(large, REL_BUCKETS - 1)
    return np.where(n < max_exact, n, large).astype(np.int32)


def _bias_table_kernel(rel_ref, bucket_ref, o_ref):
    h = pl.program_id(0)
    buckets = bucket_ref[...]
    table = jnp.zeros(buckets.shape, F32)
    for b in range(REL_BUCKETS):
        table = jnp.where(buckets == b, rel_ref[b, h], table)
    o_ref[0] = table


def _bias_table(rel_bias):
    c = SWA_WINDOW
    dist = np.arange(c)[:, None] + c - np.arange(2 * c)[None, :]
    buckets = jnp.asarray(_t5_buckets(dist))
    return pl.pallas_call(
        _bias_table_kernel,
        grid=(SWA_Q_HEADS,),
        in_specs=[pl.BlockSpec(memory_space=pltpu.SMEM),
                  pl.BlockSpec((c, 2 * c), lambda h: (0, 0))],
        out_specs=pl.BlockSpec((1, c, 2 * c), lambda h: (h, 0, 0)),
        out_shape=jax.ShapeDtypeStruct((SWA_Q_HEADS, c, 2 * c), F32),
        compiler_params=_params("arbitrary"),
    )(rel_bias, buckets)


def _swa_kernel(q_ref, kp_ref, kc_ref, vp_ref, vc_ref, bias_ref, sink_ref, o_ref, *, blocks_per_seq):
    c = SWA_WINDOW
    hd = SWA_HEAD_DIM
    has_prev = (pl.program_id(0) % blocks_per_seq) > 0
    ii = lax.broadcasted_iota(jnp.int32, (c, 2 * c), 0)
    jj = lax.broadcasted_iota(jnp.int32, (c, 2 * c), 1)
    ok = ((jj < c) & (jj > ii) & has_prev) | ((jj >= c) & (jj - c <= ii))
    left = lax.broadcasted_iota(jnp.int32, (c, LANES), 1) < hd
    ones = jnp.ones((2 * c, LANES), BF16)
    zeros = jnp.zeros((2 * c, hd), BF16)
    scale = hd ** -0.5
    for g in range(SWA_KV_HEADS):
        cols = slice(g * hd, (g + 1) * hd)
        k = jnp.concatenate([kp_ref[:, cols], kc_ref[:, cols]], axis=0)
        v = jnp.concatenate([vp_ref[:, cols], vc_ref[:, cols]], axis=0)
        k_half = (jnp.concatenate([k, zeros], axis=1), jnp.concatenate([zeros, k], axis=1))
        v_half = (jnp.concatenate([v, zeros], axis=1), jnp.concatenate([zeros, v], axis=1))
        heads = range(g * SWA_GROUP, (g + 1) * SWA_GROUP)
        s = [_dot_nt(q_ref[:, (h // 2) * LANES:(h // 2 + 1) * LANES], k_half[h % 2]) for h in heads]
        s = [jnp.where(ok, s[i] * scale + bias_ref[h], NEG_INF) for i, h in enumerate(heads)]
        mx = [jnp.maximum(jnp.max(s[i], axis=-1, keepdims=True), sink_ref[h])
              for i, h in enumerate(heads)]
        p = [jnp.exp(s[i] - mx[i]).astype(BF16) for i in range(SWA_GROUP)]
        den = [_dot(p[i], ones) + jnp.exp(sink_ref[h] - mx[i]) for i, h in enumerate(heads)]
        for i in range(0, SWA_GROUP, 2):
            num = _dot(p[i], v_half[0]) + _dot(p[i + 1], v_half[1])
            out = num / jnp.where(left, den[i], den[i + 1])
            lo = (g * SWA_GROUP + i) * hd
            o_ref[:, lo:lo + LANES] = out.astype(o_ref.dtype)


def _swa_core(qkv, bias, sinks, *, seq):
    m = qkv.shape[0]
    c = SWA_WINDOW
    blocks_per_seq = seq // c
    qw = SWA_Q_HEADS * SWA_HEAD_DIM
    kw = SWA_KV_HEADS * SWA_HEAD_DIM
    kcol = qw // kw
    prev = lambda r: jnp.where(r % blocks_per_seq == 0, r, r - 1)
    return pl.pallas_call(
        functools.partial(_swa_kernel, blocks_per_seq=blocks_per_seq),
        grid=(m // c,),
        in_specs=[pl.BlockSpec((c, qw), lambda r: (r, 0)),
                  pl.BlockSpec((c, kw), lambda r: (prev(r), kcol)),
                  pl.BlockSpec((c, kw), lambda r: (r, kcol)),
                  pl.BlockSpec((c, kw), lambda r: (prev(r), kcol + 1)),
                  pl.BlockSpec((c, kw), lambda r: (r, kcol + 1)),
                  pl.BlockSpec((SWA_Q_HEADS, c, 2 * c), lambda r: (0, 0, 0)),
                  pl.BlockSpec(memory_space=pltpu.SMEM)],
        out_specs=pl.BlockSpec((c, qw), lambda r: (r, 0)),
        out_shape=jax.ShapeDtypeStruct((m, qw), BF16),
        compiler_params=_params("parallel"),
    )(qkv, qkv, qkv, qkv, qkv, bias, sinks)


def _shifted_delta(x, halo_ref, at_seq_start):
    first = jnp.where(at_seq_start, jnp.zeros_like(halo_ref[SUBLANES - 1:, :]), halo_ref[SUBLANES - 1:, :])
    rows = lax.broadcasted_iota(jnp.int32, x.shape, 0)
    prev = jnp.where(rows == 0, first, pltpu.roll(x, 1, axis=0))
    return prev - x


def _rwkv_rkv_kernel(x_ref, halo_ref, mix_ref, w_ref, o_ref, amix_ref, *, tiles_per_seq, nj):
    i = pl.program_id(0)
    j = pl.program_id(1)

    @pl.when(j % nj == 0)
    def _():
        x = x_ref[...]
        xx = _shifted_delta(x, halo_ref, i % tiles_per_seq == 0)
        amix_ref[...] = (x + xx * mix_ref[0]).astype(BF16)

    o_ref[...] = _dot(amix_ref[...], w_ref[...])


def _rwkv_rkv(x, mix3, w3, *, seq, tm=512, tn=1024):
    m, d = x.shape
    nj = d // tn
    halo_blocks = tm // SUBLANES
    return pl.pallas_call(
        functools.partial(_rwkv_rkv_kernel, tiles_per_seq=seq // tm, nj=nj),
        grid=(m // tm, 3 * nj),
        in_specs=[pl.BlockSpec((tm, d), lambda i, j: (i, 0)),
                  pl.BlockSpec((SUBLANES, d), lambda i, j: (jnp.maximum(i * halo_blocks - 1, 0), 0)),
                  pl.BlockSpec((1, 1, d), lambda i, j: (j // nj, 0, 0)),
                  pl.BlockSpec((None, d, tn), lambda i, j: (j // nj, 0, j % nj))],
        out_specs=pl.BlockSpec((None, tm, tn), lambda i, j: (j // nj, i, j % nj)),
        out_shape=jax.ShapeDtypeStruct((3, m, d), F32),
        scratch_shapes=[pltpu.VMEM((tm, d), BF16)],
        compiler_params=_params("parallel", "arbitrary"),
    )(x, x, mix3, w3)


def _rwkv_lora_kernel(x_ref, halo_ref, mix_ref, w0_ref, a0_ref, w1_ref, w2_ref, a1_ref, a2_ref,
                      g1_ref, g2_ref, ld_ref, a_ref, g_ref, *, tiles_per_seq):
    i = pl.program_id(0)
    x = x_ref[...]
    xx = _shifted_delta(x, halo_ref, i % tiles_per_seq == 0)
    mix = mix_ref[...]
    xw = (x + xx * mix[0:1]).astype(BF16)
    xa = (x + xx * mix[1:2]).astype(BF16)
    xg = (x + xx * mix[2:3]).astype(BF16)
    z = w0_ref[...] + _dot(jnp.tanh(_dot(xw, w1_ref[...])).astype(BF16), w2_ref[...])
    ld_ref[...] = -math.exp(-0.5) * _sigmoid(z)
    a_ref[...] = _sigmoid(a0_ref[...] + _dot(_dot(xa, a1_ref[...]).astype(BF16), a2_ref[...]))
    g_ref[...] = _dot(_sigmoid(_dot(xg, g1_ref[...])).astype(BF16), g2_ref[...]).astype(g_ref.dtype)


def _rwkv_lora(x, mix3, w0, a0, w1, w2, a1, a2, g1, g2, *, seq, tm=512):
    m, d = x.shape
    halo_blocks = tm // SUBLANES
    row = lambda i: (i, 0)
    full = lambda a: pl.BlockSpec(a.shape, lambda i: (0, 0))
    return pl.pallas_call(
        functools.partial(_rwkv_lora_kernel, tiles_per_seq=seq // tm),
        grid=(m // tm,),
        in_specs=[pl.BlockSpec((tm, d), row),
                  pl.BlockSpec((SUBLANES, d), lambda i: (jnp.maximum(i * halo_blocks - 1, 0), 0)),
                  full(mix3), full(w0), full(a0), full(w1), full(w2), full(a1), full(a2),
                  full(g1), full(g2)],
        out_specs=[pl.BlockSpec((tm, d), row)] * 3,
        out_shape=[jax.ShapeDtypeStruct((m, d), F32), jax.ShapeDtypeStruct((m, d), F32),
                   jax.ShapeDtypeStruct((m, d), BF16)],
        compiler_params=_params("parallel"),
    )(x, x, mix3, w0, a0, w1, w2, a1, a2, g1, g2)


def _pair_mask(n):
    rows = lax.broadcasted_iota(jnp.int32, (2 * n, 2 * n), 0)
    cols = lax.broadcasted_iota(jnp.int32, (2 * n, 2 * n), 1)
    return (rows < n) == (cols < n)


def _block_diag(y, same_head):
    return jnp.where(same_head, jnp.concatenate([y, y], axis=0), 0.0).astype(BF16)


def _fold_diag(x, n, left):
    return jnp.where(left, x[:n], x[n:])


def _segment_sum(x, ones_bd):
    hi = x.astype(BF16)
    lo = (x - hi.astype(F32)).astype(BF16)
    return _dot(hi, ones_bd) + _dot(lo, ones_bd)


def _chunk_cumsum(x, n):
    pos = lax.broadcasted_iota(jnp.int32, x.shape, 0) % n
    shift = 1
    while shift < n:
        x = x + jnp.where(pos >= shift, pltpu.roll(x, shift, axis=0), 0.0)
        shift *= 2
    return x


def _rwkv_core_kernel(r_ref, k_ref, v_ref, ld_ref, a_ref, g_ref, par_ref, o_ref, state_ref, *, tc):
    hd = RWKV_HEAD_DIM
    n = RWKV_CHUNK
    assert n == hd and 2 * hd == LANES
    chunks = range(tc // n)
    bf = lambda t: t.astype(BF16)

    @pl.when(pl.program_id(2) == 0)
    def _():
        state_ref[...] = jnp.zeros_like(state_ref)

    same_head = _pair_mask(n)
    ones_bd = jnp.where(same_head, 1.0, 0.0).astype(BF16)
    lane = lax.broadcasted_iota(jnp.int32, (n, LANES), 1)
    row = lax.broadcasted_iota(jnp.int32, (n, LANES), 0)
    left = lane < hd
    pos = lane % hd
    strict = row > pos
    incl = row >= pos
    diag = row == pos

    par = par_ref[...]
    kk_w, ka_w, rk_w, gain, bias = (par[i:i + 1] for i in range(5))
    r = r_ref[...]
    k = k_ref[...]
    v = v_ref[...]
    ld = ld_ref[...]
    a = a_ref[...]

    kk = k * kk_w
    kk = kk / jnp.maximum(jnp.sqrt(_segment_sum(kk * kk, ones_bd)), 1e-12)
    km = k * (1.0 + (a - 1.0) * ka_w)
    bv = kk * a
    cw = _chunk_cumsum(ld, n)
    w_inv = jnp.exp(-cw)
    a_t = -kk * jnp.exp(cw - ld)
    r_t = r * jnp.exp(cw)
    b_t = bv * w_inv
    k_t = km * w_inv
    bonus = _segment_sum(r * km * rk_w, ones_bd) * v

    rows = [slice(c * n, (c + 1) * n) for c in chunks]
    cw_last = [cw[(c + 1) * n - 1:(c + 1) * n] for c in chunks]
    to_end = [jnp.exp(cw_last[c] - cw[rows[c]]) for c in chunks]
    bk_e = [bf(jnp.concatenate([bv[rows[c]] * to_end[c], km[rows[c]] * to_end[c]], axis=0))
            for c in chunks]
    zeros = jnp.zeros((n, LANES), F32)
    zeros_bd = jnp.zeros((2 * n, 2 * n), BF16)
    bd = lambda t: _block_diag(t, same_head)

    x = [_dot_nt(bf(jnp.concatenate([a_t[rows[c]], r_t[rows[c]]], axis=0)),
                 jnp.concatenate([bd(b_t[rows[c]]), bd(k_t[rows[c]])], axis=0)) for c in chunks]
    a_ab = [jnp.where(strict, x[c][:n, :LANES], 0.0) for c in chunks]
    a_ak = [jnp.where(strict, x[c][:n, LANES:], 0.0) for c in chunks]
    a_rbk = [bf(jnp.concatenate([jnp.where(incl, x[c][n:, :LANES], 0.0),
                                 jnp.where(incl, x[c][n:, LANES:], 0.0)], axis=1)) for c in chunks]

    inv = [jnp.where(diag, 1.0, a_ab[c]) for c in chunks]
    power = [_dot(bf(a_ab[c]), bd(a_ab[c])) for c in chunks]
    for _ in range(int(math.log2(n)) - 2):
        both = [_dot(bf(jnp.concatenate([inv[c], power[c]], axis=0)), bd(power[c])) for c in chunks]
        inv = [inv[c] + both[c][:n] for c in chunks]
        power = [both[c][n:] for c in chunks]
    inv = [inv[c] + _dot(bf(inv[c]), bd(power[c])) for c in chunks]

    av = [_dot(bf(a_ak[c]), bd(v[rows[c]])) for c in chunks]
    au = [_dot(bf(inv[c]), jnp.concatenate([bd(a_t[rows[c]]), bd(av[c])], axis=1))
          for c in chunks]
    gc = [_dot_tn(bk_e[c], bf(jnp.concatenate(
              [au[c], jnp.concatenate([zeros, v[rows[c]]], axis=1)], axis=0))) for c in chunks]
    g_mat = [_fold_diag(gc[c][:, :LANES], n, left) + jnp.where(diag, jnp.exp(cw_last[c]), 0.0)
             for c in chunks]
    c0 = [_fold_diag(gc[c][:, LANES:], n, left) for c in chunks]
    ry = [_dot(a_rbk[c], jnp.concatenate(
              [jnp.concatenate([bd(au[c][:, :LANES]), bd(au[c][:, LANES:])], axis=1),
               jnp.concatenate([zeros_bd, bd(v[rows[c]])], axis=1)], axis=0))
          for c in chunks]
    rg = [bf(jnp.concatenate([r_t[rows[c]] + ry[c][:, :LANES], g_mat[c]], axis=0)) for c in chunks]

    st = state_ref[...]
    ys = []
    for c in chunks:
        out = _dot(rg[c], bd(st))
        ys.append(out[:n] + ry[c][:, LANES:])
        st = out[n:] + c0[c]
    state_ref[...] = st

    y = jnp.concatenate(ys, axis=0)
    mu = _segment_sum(y, ones_bd) * (1.0 / hd)
    yc = y - mu
    var = _segment_sum(yc * yc, ones_bd) * (1.0 / hd)
    yn = yc * lax.rsqrt(var + RWKV_GN_EPS) * gain + bias
    o_ref[...] = ((yn + bonus) * g_ref[...].astype(F32)).astype(o_ref.dtype)


def _rwkv_core(rkv, ld, a, g, par, *, batch, seq, tc=512):
    m, d = ld.shape
    nt = seq // tc
    blk = lambda b, hp, t: (b * nt + t, hp)
    sel = lambda which: pl.BlockSpec((None, tc, LANES), lambda b, hp, t: (which, b * nt + t, hp))
    return pl.pallas_call(
        functools.partial(_rwkv_core_kernel, tc=tc),
        grid=(batch, d // LANES, nt),
        in_specs=[sel(0), sel(1), sel(2),
                  pl.BlockSpec((tc, LANES), blk),
                  pl.BlockSpec((tc, LANES), blk),
                  pl.BlockSpec((tc, LANES), blk),
                  pl.BlockSpec((SUBLANES, LANES), lambda b, hp, t: (0, hp))],
        out_specs=pl.BlockSpec((tc, LANES), blk),
        out_shape=jax.ShapeDtypeStruct((m, d), BF16),
        scratch_shapes=[pltpu.VMEM((RWKV_HEAD_DIM, LANES), F32)],
        compiler_params=_params("parallel", "parallel", "arbitrary"),
    )(rkv, rkv, rkv, ld, a, g, par)


def _pad_cols(w, n):
    return jnp.pad(w, ((0, 0), (0, n - w.shape[1])))


def _pad_rows(w, n):
    return jnp.pad(w, ((0, n - w.shape[0]), (0, 0)))


def kernel(x, p, ln_gain, ln_bias, ret_w_in, ret_gn_gain, ret_w_out, swa_w_qkv, swa_sinks, swa_w_out, rel_bias, rwkv_mix, rwkv_w_rkv, rwkv_w0, rwkv_w1, rwkv_w2, rwkv_a0, rwkv_a1, rwkv_a2, rwkv_g1, rwkv_g2, rwkv_k_k, rwkv_k_a, rwkv_r_k, rwkv_gn_gain, rwkv_gn_bias, rwkv_w_out, ffn_w_up, ffn_conv_w, ffn_conv_b, ffn_w_down, ple_w_proj, ple_w_gate):
    batch, seq, d = x.shape
    m = batch * seq
    xf = x.reshape(m, d)
    xb = xf.astype(BF16)
    row = lambda v: v.reshape(1, -1)

    half = RET_QK_DIM // 2
    inv_freq = 1.0 / (ROPE_BASE ** (jnp.arange(0, RET_QK_DIM, 2, dtype=F32) / RET_QK_DIM))
    ang = jnp.arange(seq, dtype=F32)[:, None] * inv_freq[None, :]
    cos, sin = jnp.cos(ang), jnp.sin(ang)
    log_gamma = jnp.log(1.0 - 2.0 ** (-5.0 - jnp.arange(RET_HEADS, dtype=F32)))
    lg = jnp.broadcast_to(log_gamma[:, None, None], (RET_HEADS, 1, RET_V_DIM))

    for i in range(DEPTH):
        kind, j = i % N_MIXERS, i // N_MIXERS
        g1, b1 = row(ln_gain[i, 0]), row(ln_bias[i, 0])
        if kind == 0:
            proj = _ret_proj(xb, ret_w_in[j].astype(BF16), cos, sin, seq=seq)
            o = _ret_core(proj, lg, row(ret_gn_gain[j]), batch=batch, seq=seq)
            xf, xb = _matmul_ln(o, ret_w_out[j].astype(BF16), xf, g1, b1)
        elif kind == 1:
            qkv = _matmul(xb, swa_w_qkv[j].astype(BF16), tm=1024, tn=1280, out_dtype=BF16)
            o = _swa_core(qkv, _bias_table(rel_bias), swa_sinks[j], seq=seq)
            xf, xb = _matmul_ln(o, swa_w_out[j].astype(BF16), xf, g1, b1)
        else:
            mix = rwkv_mix[j]
            rkv = _rwkv_rkv(xf, mix[jnp.array([0, 2, 3])].reshape(3, 1, d),
                            rwkv_w_rkv[j].astype(BF16), seq=seq)
            lora = RWKV_HEAD_DIM * 2
            ld, a, g = _rwkv_lora(
                xf, mix[jnp.array([1, 4, 5])], row(rwkv_w0[j]), row(rwkv_a0[j]),
                _pad_cols(rwkv_w1[j], lora).astype(BF16), _pad_rows(rwkv_w2[j], lora).astype(BF16),
                _pad_cols(rwkv_a1[j], lora).astype(BF16), _pad_rows(rwkv_a2[j], lora).astype(BF16),
                rwkv_g1[j].astype(BF16), rwkv_g2[j].astype(BF16), seq=seq)
            par = jnp.stack([rwkv_k_k[j], rwkv_k_a[j], rwkv_r_k[j].reshape(-1), rwkv_gn_gain[j],
                             rwkv_gn_bias[j]] + [jnp.zeros((d,), F32)] * 3)
            o = _rwkv_core(rkv, ld, a, g, par, batch=batch, seq=seq)
            xf, xb = _matmul_ln(o, rwkv_w_out[j].astype(BF16), xf, g1, b1)

        w_up = ffn_w_up[i]
        conv = jnp.concatenate([ffn_conv_w[i], ffn_conv_b[i][None]], axis=0)
        conv = jnp.pad(conv, ((0, SUBLANES - conv.shape[0]), (0, 0)))
        xf, xb = _ffn(xb, xf,
                      _pad_cols(w_up[:, :D_FF], D_FF_PAD).astype(BF16),
                      _pad_cols(w_up[:, D_FF:], D_FF_PAD).astype(BF16),
                      _pad_cols(conv[:, :D_FF], D_FF_PAD), _pad_cols(conv[:, D_FF:], D_FF_PAD),
                      _pad_rows(ffn_w_down[i], D_FF_PAD).astype(BF16),
                      row(ln_gain[i, 1]), row(ln_bias[i, 1]), seq=seq)
        xf, xb = _ple(xb, xf, p[i].reshape(m, PLE_DIM), ple_w_gate[i].astype(BF16),
                      ple_w_proj[i].astype(BF16))
    return xf.reshape(batch, seq, d)
```

```python
import functools
import math

import numpy as np
import jax
import jax.numpy as jnp
from jax import lax
from jax.experimental import pallas as pl
from jax.experimental.pallas import tpu as pltpu

F32 = jnp.float32
BF16 = jnp.bfloat16

D_MODEL = 2048
DEPTH = 4
N_MIXERS = 3

RET_HEADS = 8
RET_QK_DIM = D_MODEL // RET_HEADS
RET_V_DIM = 2 * D_MODEL // RET_HEADS
RET_CHUNK = 128
RET_GN_EPS = 1e-5
ROPE_BASE = 10000.0

SWA_HEAD_DIM = 64
SWA_Q_HEADS = D_MODEL // SWA_HEAD_DIM
SWA_KV_HEADS = SWA_Q_HEADS // 8
SWA_GROUP = SWA_Q_HEADS // SWA_KV_HEADS
SWA_WINDOW = 128
REL_BUCKETS = 32
REL_MAX_DIST = SWA_WINDOW
NEG_INF = -1e30

RWKV_HEAD_DIM = 64
RWKV_HEADS = D_MODEL // RWKV_HEAD_DIM
RWKV_GN_EPS = 64e-5
RWKV_CHUNK = 64

D_FF = 5504
CONV_WIDTH = 3
PLE_DIM = 256

LN_EPS = 1e-5
DEEPNORM_ALPHA = (2.0 * DEPTH) ** 0.25

LANES = 128
SUBLANES = 8
BF16_ROWS = 16
VMEM_LIMIT_BYTES = 56 * 1024 * 1024
FFN_COL_TILE = 512
D_FF_PAD = -(-D_FF // FFN_COL_TILE) * FFN_COL_TILE


def _params(*semantics, flags=None):
    return pltpu.CompilerParams(dimension_semantics=semantics,
                                vmem_limit_bytes=VMEM_LIMIT_BYTES, flags=flags)


def _dot(a, b):
    return jnp.dot(a, b, preferred_element_type=F32)


def _dot_nt(a, b):
    return lax.dot_general(a, b, (((1,), (1,)), ((), ())), preferred_element_type=F32)


def _dot_tn(a, b):
    return lax.dot_general(a, b, (((0,), (0,)), ((), ())), preferred_element_type=F32)


def _layer_norm_rows(z, gain, bias):
    mu = jnp.mean(z, axis=-1, keepdims=True)
    zc = z - mu
    var = jnp.mean(zc * zc, axis=-1, keepdims=True)
    return zc * lax.rsqrt(var + LN_EPS) * gain + bias


def _sigmoid(x):
    return 1.0 / (1.0 + jnp.exp(-x))


def _silu(x):
    return x * _sigmoid(x)


def _matmul_kernel(a_ref, w_ref, o_ref):
    o_ref[...] = _dot(a_ref[...], w_ref[...]).astype(o_ref.dtype)


def _col_tiles(w, tn):
    k, n = w.shape
    return w.reshape(k, n // tn, tn).transpose(1, 0, 2)


def _col_tile_spec(k, tn):
    return pl.BlockSpec((None, k, tn), lambda i, j: (j, 0, 0))


def _matmul(a, w, *, tm, out_dtype):
    m, k = a.shape
    nt, _, tn = w.shape
    n = nt * tn
    return pl.pallas_call(
        _matmul_kernel,
        grid=(m // tm, nt),
        in_specs=[pl.BlockSpec((tm, k), lambda i, j: (i, 0)),
                  _col_tile_spec(k, tn)],
        out_specs=pl.BlockSpec((tm, tn), lambda i, j: (i, j)),
        out_shape=jax.ShapeDtypeStruct((m, n), out_dtype),
        compiler_params=_params("parallel", "arbitrary"),
    )(a, w)


def _matmul_ln_kernel(a_ref, w_ref, res_ref, gain_ref, bias_ref, xo_ref, xb_ref, *scratch, nk):
    part = _dot(a_ref[...], w_ref[...])

    def finalize(y):
        xn = _layer_norm_rows(DEEPNORM_ALPHA * res_ref[...] + y, gain_ref[...], bias_ref[...])
        xo_ref[...] = xn
        xb_ref[...] = xn.astype(BF16)

    if nk == 1:
        finalize(part)
        return
    acc_ref, = scratch
    k = pl.program_id(1)

    @pl.when(k == 0)
    def _():
        acc_ref[...] = part

    @pl.when((k > 0) & (k < nk - 1))
    def _():
        acc_ref[...] += part

    @pl.when(k == nk - 1)
    def _():
        finalize(acc_ref[...] + part)


def _matmul_ln(a, w, res, gain, bias, *, tm=512, tk=2048):
    m, kdim = a.shape
    d = w.shape[1]
    nk = kdim // tk
    row = lambda i, k: (i, 0)
    return pl.pallas_call(
        functools.partial(_matmul_ln_kernel, nk=nk),
        grid=(m // tm, nk),
        in_specs=[pl.BlockSpec((tm, tk), lambda i, k: (i, k)),
                  pl.BlockSpec((tk, d), lambda i, k: (k, 0)),
                  pl.BlockSpec((tm, d), row),
                  pl.BlockSpec((1, d), lambda i, k: (0, 0)),
                  pl.BlockSpec((1, d), lambda i, k: (0, 0))],
        out_specs=[pl.BlockSpec((tm, d), row), pl.BlockSpec((tm, d), row)],
        out_shape=[jax.ShapeDtypeStruct((m, d), F32), jax.ShapeDtypeStruct((m, d), BF16)],
        scratch_shapes=[pltpu.VMEM((tm, d), F32)] if nk > 1 else [],
        compiler_params=_params("parallel", "arbitrary"),
    )(a, w, res, gain, bias)


def _ffn_kernel(xb_ref, halo_ref, res_ref, w_ref, c_ref, wd_ref,
                gain_ref, bias_ref, xo_ref, xob_ref, acc_ref, *, tm, tiles_per_seq, nf, tf, sub, row_split):
    i = pl.program_id(0)
    f = pl.program_id(1)
    halo = jnp.where(i % tiles_per_seq == 0, jnp.zeros_like(halo_ref[...]), halo_ref[...])
    x = jnp.concatenate([halo, xb_ref[...]], axis=0)

    def conv_branch(xr, cols):
        h = _dot(xr, w_ref[:, cols])
        c = c_ref[:, cols]
        out = c[3:4] + h[BF16_ROWS:] * c[2:3]
        out = out + pltpu.roll(h, 1, axis=0)[BF16_ROWS:] * c[1:2]
        out = out + pltpu.roll(h, 2, axis=0)[BF16_ROWS:] * c[0:1]
        return out

    @pl.when(f == 0)
    def _():
        acc_ref[...] = jnp.zeros_like(acc_ref)

    rows = tm // row_split
    for rb in range(row_split):
        xr = x[rb * rows:(rb + 1) * rows + BF16_ROWS]
        acts = []
        for s in range(tf // sub):
            u = conv_branch(xr, slice(s * sub, (s + 1) * sub))
            gate = conv_branch(xr, slice(tf + s * sub, tf + (s + 1) * sub))
            acts.append((_silu(gate) * u).astype(BF16))
        acc_ref[rb * rows:(rb + 1) * rows, :] += _dot(jnp.concatenate(acts, axis=1), wd_ref[...])

    @pl.when(f == nf - 1)
    def _():
        xn = _layer_norm_rows(DEEPNORM_ALPHA * res_ref[...] + acc_ref[...], gain_ref[...],
                              bias_ref[...])
        xo_ref[...] = xn
        xob_ref[...] = xn.astype(BF16)


def _ffn(xb, xres, w_ug, c_ug, wd, gain, bias, *, seq, tm=512, sub=256, row_split=2):
    m, d = xb.shape
    nf = w_ug.shape[0]
    tf = w_ug.shape[2] // 2
    halo_blocks = tm // BF16_ROWS
    row = lambda i, f: (i, 0)
    const = lambda i, f: (0, 0)
    once = pl.Buffered(1)
    return pl.pallas_call(
        functools.partial(_ffn_kernel, tm=tm, tiles_per_seq=seq // tm, nf=nf, tf=tf, sub=sub,
                          row_split=row_split),
        grid=(m // tm, nf),
        in_specs=[pl.BlockSpec((tm, d), row),
                  pl.BlockSpec((BF16_ROWS, d), lambda i, f: (jnp.maximum(i * halo_blocks - 1, 0), 0)),
                  pl.BlockSpec((tm, d), row, pipeline_mode=once),
                  _col_tile_spec(d, 2 * tf),
                  _col_tile_spec(SUBLANES, 2 * tf),
                  pl.BlockSpec((tf, d), lambda i, f: (f, 0)),
                  pl.BlockSpec((1, d), const),
                  pl.BlockSpec((1, d), const)],
        out_specs=[pl.BlockSpec((tm, d), row, pipeline_mode=once),
                   pl.BlockSpec((tm, d), row, pipeline_mode=once)],
        out_shape=[jax.ShapeDtypeStruct((m, d), F32), jax.ShapeDtypeStruct((m, d), BF16)],
        scratch_shapes=[pltpu.VMEM((tm, d), F32)],
        compiler_params=_params("parallel", "arbitrary"),
    )(xb, xb, xres, w_ug, c_ug, wd, gain, bias)


def _ple_kernel(xb_ref, wg_ref, p_ref, wp_ref, res_ref, xo_ref, xob_ref):
    gate = _sigmoid(_dot(xb_ref[...], wg_ref[...]))
    proj = _dot(p_ref[...].astype(BF16), wp_ref[...])
    xn = res_ref[...] + proj * gate
    xo_ref[...] = xn
    xob_ref[...] = xn.astype(BF16)


def _ple(xb, xres, p, wg, wp, *, tm=1024):
    m, d = xb.shape
    pd = p.shape[1]
    tn = wg.shape[2]
    return pl.pallas_call(
        _ple_kernel,
        grid=(m // tm, d // tn),
        in_specs=[pl.BlockSpec((tm, d), lambda i, j: (i, 0)),
                  _col_tile_spec(d, tn),
                  pl.BlockSpec((tm, pd), lambda i, j: (i, 0)),
                  _col_tile_spec(pd, tn),
                  pl.BlockSpec((tm, tn), lambda i, j: (i, j))],
        out_specs=[pl.BlockSpec((tm, tn), lambda i, j: (i, j)),
                   pl.BlockSpec((tm, tn), lambda i, j: (i, j))],
        out_shape=[jax.ShapeDtypeStruct((m, d), F32), jax.ShapeDtypeStruct((m, d), BF16)],
        compiler_params=_params("parallel", "arbitrary"),
    )(xb, wg, p, wp, xres)


def _ret_proj_kernel(a_ref, w_ref, cos_ref, sin_ref, o_ref, *, tn, nq, nkv):
    j = pl.program_id(1)
    acc = _dot(a_ref[...], w_ref[...])
    half = RET_QK_DIM // 2

    def rotary(scale):
        cos = cos_ref[...]
        sin = sin_ref[...]
        for h in range(tn // RET_QK_DIM):
            lo = h * RET_QK_DIM
            x1 = acc[:, lo:lo + half]
            x2 = acc[:, lo + half:lo + RET_QK_DIM]
            o_ref[:, lo:lo + half] = ((x1 * cos - x2 * sin) * scale).astype(o_ref.dtype)
            o_ref[:, lo + half:lo + RET_QK_DIM] = ((x1 * sin + x2 * cos) * scale).astype(o_ref.dtype)

    @pl.when(j < nq)
    def _():
        rotary(1.0)

    @pl.when((j >= nq) & (j < 2 * nq))
    def _():
        rotary(RET_QK_DIM ** -0.5)

    @pl.when((j >= 2 * nq) & (j < 2 * nq + nkv))
    def _():
        o_ref[...] = acc.astype(o_ref.dtype)

    @pl.when(j >= 2 * nq + nkv)
    def _():
        o_ref[...] = _silu(acc).astype(o_ref.dtype)


def _ret_proj(xb, w, cos, sin, *, seq, tm=1024):
    m, d = xb.shape
    nt, _, tn = w.shape
    n = nt * tn
    nq = RET_HEADS * RET_QK_DIM // tn
    nkv = RET_HEADS * RET_V_DIM // tn
    tiles_per_seq = seq // tm
    return pl.pallas_call(
        functools.partial(_ret_proj_kernel, tn=tn, nq=nq, nkv=nkv),
        grid=(m // tm, nt),
        in_specs=[pl.BlockSpec((tm, d), lambda i, j: (i, 0)),
                  _col_tile_spec(d, tn),
                  pl.BlockSpec((tm, RET_QK_DIM // 2), lambda i, j: (i % tiles_per_seq, 0)),
                  pl.BlockSpec((tm, RET_QK_DIM // 2), lambda i, j: (i % tiles_per_seq, 0))],
        out_specs=pl.BlockSpec((tm, tn), lambda i, j: (i, j)),
        out_shape=jax.ShapeDtypeStruct((m, n), BF16),
        compiler_params=_params("parallel", "arbitrary"),
    )(xb, w, cos, sin)


def _ret_core_kernel(q_ref, k_ref, v_ref, g_ref, lg_ref, gain_ref, o_ref, state_ref, *, ts):
    c = RET_CHUNK

    @pl.when(pl.program_id(2) == 0)
    def _():
        state_ref[...] = jnp.zeros_like(state_ref)

    lg = lg_ref[0]
    ii = lax.broadcasted_iota(jnp.int32, (c, c), 0)
    jj = lax.broadcasted_iota(jnp.int32, (c, c), 1)
    diff = (ii - jj).astype(F32)
    decay_mask = jnp.where(ii >= jj, jnp.exp(lg[:, :c] * jnp.maximum(diff, 0.0)), 0.0)
    row_v = lax.broadcasted_iota(jnp.int32, (c, RET_V_DIM), 0).astype(F32)
    row_k = lax.broadcasted_iota(jnp.int32, (c, RET_QK_DIM), 0).astype(F32)
    q_decay = jnp.exp(lg * (row_v + 1.0))
    k_decay = jnp.exp(lg[:, :RET_QK_DIM] * (c - 1.0 - row_k))
    chunk_decay = jnp.exp(lg * float(c))
    gain = gain_ref[...]

    chunks = range(ts // c)
    rows = [slice(n * c, (n + 1) * c) for n in chunks]
    scores = [(_dot_nt(q_ref[rows[n], :], k_ref[rows[n], :]) * decay_mask).astype(BF16)
              for n in chunks]
    inner = [_dot(scores[n], v_ref[rows[n], :]) for n in chunks]
    kv = [_dot_tn((k_ref[rows[n], :].astype(F32) * k_decay).astype(BF16), v_ref[rows[n], :])
          for n in chunks]
    state = state_ref[...]
    states = []
    for n in chunks:
        states.append(state.astype(BF16))
        state = state * chunk_decay + kv[n]
    state_ref[...] = state
    cross = [_dot(q_ref[rows[n], :], states[n]) * q_decay for n in chunks]
    for n in chunks:
        o = inner[n] + cross[n]
        mu = jnp.mean(o, axis=-1, keepdims=True)
        oc = o - mu
        var = jnp.mean(oc * oc, axis=-1, keepdims=True)
        on = oc * lax.rsqrt(var + RET_GN_EPS) * gain
        o_ref[rows[n], :] = (g_ref[rows[n], :].astype(F32) * on).astype(o_ref.dtype)


def _ret_core(proj, lg, gn_gain, *, batch, seq, ts=1024):
    m = proj.shape[0]
    nt = seq // ts
    kq = RET_HEADS
    row = lambda b, h, t: b * nt + t
    return pl.pallas_call(
        functools.partial(_ret_core_kernel, ts=ts),
        grid=(batch, RET_HEADS, nt),
        in_specs=[pl.BlockSpec((ts, RET_QK_DIM), lambda b, h, t: (row(b, h, t), h)),
                  pl.BlockSpec((ts, RET_QK_DIM), lambda b, h, t: (row(b, h, t), kq + h)),
                  pl.BlockSpec((ts, RET_V_DIM), lambda b, h, t: (row(b, h, t), kq + h)),
                  pl.BlockSpec((ts, RET_V_DIM), lambda b, h, t: (row(b, h, t), 2 * kq + h)),
                  pl.BlockSpec((1, 1, RET_V_DIM), lambda b, h, t: (h, 0, 0)),
                  pl.BlockSpec((1, RET_V_DIM), lambda b, h, t: (0, h))],
        out_specs=pl.BlockSpec((ts, RET_V_DIM), lambda b, h, t: (row(b, h, t), h)),
        out_shape=jax.ShapeDtypeStruct((m, RET_HEADS * RET_V_DIM), BF16),
        scratch_shapes=[pltpu.VMEM((RET_QK_DIM, RET_V_DIM), F32)],
        compiler_params=_params("parallel", "parallel", "arbitrary"),
    )(proj, proj, proj, proj, lg, gn_gain)


def _t5_buckets(dist):
    n = np.maximum(dist, 0)
    max_exact = REL_BUCKETS // 2
    large = max_exact + (np.log(np.maximum(n, 1) / max_exact) / np.log(REL_MAX_DIST / max_exact)
                         * (REL_BUCKETS - max_exact)).astype(np.int32)
    large = np.minimum(large, REL_BUCKETS - 1)
    return np.where(n < max_exact, n, large).astype(np.int32)


def _bias_table_kernel(rel_ref, bucket_ref, o_ref):
    h = pl.program_id(0)
    buckets = bucket_ref[...]
    table = jnp.zeros(buckets.shape, F32)
    for b in range(REL_BUCKETS):
        table = jnp.where(buckets == b, rel_ref[b, h], table)
    o_ref[0] = table


def _bias_table(rel_bias):
    c = SWA_WINDOW
    dist = np.arange(c)[:, None] + c - np.arange(2 * c)[None, :]
    buckets = jnp.asarray(_t5_buckets(dist))
    return pl.pallas_call(
        _bias_table_kernel,
        grid=(SWA_Q_HEADS,),
        in_specs=[pl.BlockSpec(memory_space=pltpu.SMEM),
                  pl.BlockSpec((c, 2 * c), lambda h: (0, 0))],
        out_specs=pl.BlockSpec((1, c, 2 * c), lambda h: (h, 0, 0)),
        out_shape=jax.ShapeDtypeStruct((SWA_Q_HEADS, c, 2 * c), F32),
        compiler_params=_params("arbitrary"),
    )(rel_bias, buckets)


def _swa_kernel(q_ref, kp_ref, kc_ref, vp_ref, vc_ref, bias_ref, sink_ref, o_ref, *, blocks_per_seq):
    c = SWA_WINDOW
    hd = SWA_HEAD_DIM
    has_prev = (pl.program_id(0) % blocks_per_seq) > 0
    ii = lax.broadcasted_iota(jnp.int32, (c, 2 * c), 0)
    jj = lax.broadcasted_iota(jnp.int32, (c, 2 * c), 1)
    ok = ((jj < c) & (jj > ii) & has_prev) | ((jj >= c) & (jj - c <= ii))
    left = lax.broadcasted_iota(jnp.int32, (c, LANES), 1) < hd
    ones = jnp.ones((2 * c, LANES), BF16)
    zeros = jnp.zeros((2 * c, hd), BF16)
    scale = hd ** -0.5
    for g in range(SWA_KV_HEADS):
        cols = slice(g * hd, (g + 1) * hd)
        k = jnp.concatenate([kp_ref[:, cols], kc_ref[:, cols]], axis=0)
        v = jnp.concatenate([vp_ref[:, cols], vc_ref[:, cols]], axis=0)
        k_half = (jnp.concatenate([k, zeros], axis=1), jnp.concatenate([zeros, k], axis=1))
        v_half = (jnp.concatenate([v, zeros], axis=1), jnp.concatenate([zeros, v], axis=1))
        heads = range(g * SWA_GROUP, (g + 1) * SWA_GROUP)
        s = [_dot_nt(q_ref[:, (h // 2) * LANES:(h // 2 + 1) * LANES], k_half[h % 2]) for h in heads]
        s = [jnp.where(ok, s[i] * scale + bias_ref[h], NEG_INF) for i, h in enumerate(heads)]
        mx = [jnp.maximum(jnp.max(s[i], axis=-1, keepdims=True), sink_ref[h])
              for i, h in enumerate(heads)]
        p = [jnp.exp(s[i] - mx[i]).astype(BF16) for i in range(SWA_GROUP)]
        den = [_dot(p[i], ones) + jnp.exp(sink_ref[h] - mx[i]) for i, h in enumerate(heads)]
        for i in range(0, SWA_GROUP, 2):
            num = _dot(p[i], v_half[0]) + _dot(p[i + 1], v_half[1])
            out = num / jnp.where(left, den[i], den[i + 1])
            lo = (g * SWA_GROUP + i) * hd
            o_ref[:, lo:lo + LANES] = out.astype(o_ref.dtype)


def _swa_core(qkv, bias, sinks, *, seq):
    m = qkv.shape[0]
    c = SWA_WINDOW
    blocks_per_seq = seq // c
    qw = SWA_Q_HEADS * SWA_HEAD_DIM
    kw = SWA_KV_HEADS * SWA_HEAD_DIM
    kcol = qw // kw
    prev = lambda r: jnp.where(r % blocks_per_seq == 0, r, r - 1)
    return pl.pallas_call(
        functools.partial(_swa_kernel, blocks_per_seq=blocks_per_seq),
        grid=(m // c,),
        in_specs=[pl.BlockSpec((c, qw), lambda r: (r, 0)),
                  pl.BlockSpec((c, kw), lambda r: (prev(r), kcol)),
                  pl.BlockSpec((c, kw), lambda r: (r, kcol)),
                  pl.BlockSpec((c, kw), lambda r: (prev(r), kcol + 1)),
                  pl.BlockSpec((c, kw), lambda r: (r, kcol + 1)),
                  pl.BlockSpec((SWA_Q_HEADS, c, 2 * c), lambda r: (0, 0, 0)),
                  pl.BlockSpec(memory_space=pltpu.SMEM)],
        out_specs=pl.BlockSpec((c, qw), lambda r: (r, 0)),
        out_shape=jax.ShapeDtypeStruct((m, qw), BF16),
        compiler_params=_params("parallel"),
    )(qkv, qkv, qkv, qkv, qkv, bias, sinks)


def _shifted_delta(x, halo_ref, at_seq_start):
    first = jnp.where(at_seq_start, jnp.zeros_like(halo_ref[SUBLANES - 1:, :]), halo_ref[SUBLANES - 1:, :])
    rows = lax.broadcasted_iota(jnp.int32, x.shape, 0)
    prev = jnp.where(rows == 0, first, pltpu.roll(x, 1, axis=0))
    return prev - x


def _rwkv_rkv_kernel(x_ref, halo_ref, mix_ref, w_ref, o_ref, amix_ref, *, tiles_per_seq, nj):
    i = pl.program_id(0)
    j = pl.program_id(1)

    @pl.when(j % nj == 0)
    def _():
        x = x_ref[...]
        xx = _shifted_delta(x, halo_ref, i % tiles_per_seq == 0)
        amix_ref[...] = (x + xx * mix_ref[0]).astype(BF16)

    o_ref[...] = _dot(amix_ref[...], w_ref[...])


def _rwkv_rkv(x, mix3, w3, *, seq, tm=512):
    m, d = x.shape
    tn = w3.shape[2]
    nj = d // tn
    halo_blocks = tm // SUBLANES
    return pl.pallas_call(
        functools.partial(_rwkv_rkv_kernel, tiles_per_seq=seq // tm, nj=nj),
        grid=(m // tm, 3 * nj),
        in_specs=[pl.BlockSpec((tm, d), lambda i, j: (i, 0)),
                  pl.BlockSpec((SUBLANES, d), lambda i, j: (jnp.maximum(i * halo_blocks - 1, 0), 0)),
                  pl.BlockSpec((1, 1, d), lambda i, j: (j // nj, 0, 0)),
                  _col_tile_spec(d, tn)],
        out_specs=pl.BlockSpec((None, tm, tn), lambda i, j: (j // nj, i, j % nj)),
        out_shape=jax.ShapeDtypeStruct((3, m, d), F32),
        scratch_shapes=[pltpu.VMEM((tm, d), BF16)],
        compiler_params=_params("parallel", "arbitrary"),
    )(x, x, mix3, w3)


def _rwkv_lora_kernel(x_ref, halo_ref, mix_ref, w0_ref, a0_ref, w1_ref, w2_ref, a1_ref, a2_ref,
                      g1_ref, g2_ref, ld_ref, a_ref, g_ref, *, tiles_per_seq):
    i = pl.program_id(0)
    x = x_ref[...]
    xx = _shifted_delta(x, halo_ref, i % tiles_per_seq == 0)
    mix = mix_ref[...]
    xw = (x + xx * mix[0:1]).astype(BF16)
    xa = (x + xx * mix[1:2]).astype(BF16)
    xg = (x + xx * mix[2:3]).astype(BF16)
    z = w0_ref[...] + _dot(jnp.tanh(_dot(xw, w1_ref[...])).astype(BF16), w2_ref[...])
    ld_ref[...] = -math.exp(-0.5) * _sigmoid(z)
    a_ref[...] = _sigmoid(a0_ref[...] + _dot(_dot(xa, a1_ref[...]).astype(BF16), a2_ref[...]))
    g_ref[...] = _dot(_sigmoid(_dot(xg, g1_ref[...])).astype(BF16), g2_ref[...]).astype(g_ref.dtype)


def _rwkv_lora(x, mix3, w0, a0, w1, w2, a1, a2, g1, g2, *, seq, tm=512):
    m, d = x.shape
    halo_blocks = tm // SUBLANES
    row = lambda i: (i, 0)
    full = lambda a: pl.BlockSpec(a.shape, lambda i: (0, 0))
    return pl.pallas_call(
        functools.partial(_rwkv_lora_kernel, tiles_per_seq=seq // tm),
        grid=(m // tm,),
        in_specs=[pl.BlockSpec((tm, d), row),
                  pl.BlockSpec((SUBLANES, d), lambda i: (jnp.maximum(i * halo_blocks - 1, 0), 0)),
                  full(mix3), full(w0), full(a0), full(w1), full(w2), full(a1), full(a2),
                  full(g1), full(g2)],
        out_specs=[pl.BlockSpec((tm, d), row)] * 3,
        out_shape=[jax.ShapeDtypeStruct((m, d), F32), jax.ShapeDtypeStruct((m, d), F32),
                   jax.ShapeDtypeStruct((m, d), BF16)],
        compiler_params=_params("parallel"),
    )(x, x, mix3, w0, a0, w1, w2, a1, a2, g1, g2)


def _pair_mask(n):
    rows = lax.broadcasted_iota(jnp.int32, (2 * n, 2 * n), 0)
    cols = lax.broadcasted_iota(jnp.int32, (2 * n, 2 * n), 1)
    return (rows < n) == (cols < n)


def _block_diag(y, same_head):
    return jnp.where(same_head, jnp.concatenate([y, y], axis=0), 0.0).astype(BF16)


def _fold_diag(x, n, left):
    return jnp.where(left, x[:n], x[n:])


def _segment_sum(x, ones_bd):
    hi = x.astype(BF16)
    lo = (x - hi.astype(F32)).astype(BF16)
    return _dot(hi, ones_bd) + _dot(lo, ones_bd)


def _chunk_cumsum(x, n):
    pos = lax.broadcasted_iota(jnp.int32, x.shape, 0) % n
    shift = 1
    while shift < n:
        x = x + jnp.where(pos >= shift, pltpu.roll(x, shift, axis=0), 0.0)
        shift *= 2
    return x


def _rwkv_core_kernel(r_ref, k_ref, v_ref, ld_ref, a_ref, g_ref, par_ref, o_ref, state_ref, *, tc):
    hd = RWKV_HEAD_DIM
    n = RWKV_CHUNK
    assert n == hd and 2 * hd == LANES
    bf = lambda t: t.astype(BF16)

    @pl.when(pl.program_id(2) == 0)
    def _():
        state_ref[...] = jnp.zeros_like(state_ref)

    same_head = _pair_mask(n)
    ones_bd = jnp.where(same_head, 1.0, 0.0).astype(BF16)
    lane = lax.broadcasted_iota(jnp.int32, (n, LANES), 1)
    row = lax.broadcasted_iota(jnp.int32, (n, LANES), 0)
    left = lane < hd
    pos = lane % hd
    strict = row > pos
    incl = row >= pos
    diag = row == pos

    pairs = range(r_ref.shape[1] // LANES)
    par = par_ref[...]

    def prepare(p):
        lanes = slice(p * LANES, (p + 1) * LANES)
        kk_w, ka_w, rk_w = (par[i:i + 1, lanes] for i in range(3))
        r, k, v, ld, a = (ref[:, lanes] for ref in (r_ref, k_ref, v_ref, ld_ref, a_ref))
        kk = k * kk_w
        kk = kk / jnp.maximum(jnp.sqrt(_segment_sum(kk * kk, ones_bd)), 1e-12)
        km = k * (1.0 + (a - 1.0) * ka_w)
        bv = kk * a
        cw = _chunk_cumsum(ld, n)
        w_inv = jnp.exp(-cw)
        return (v, km, bv, cw, -kk * jnp.exp(cw - ld), r * jnp.exp(cw), bv * w_inv, km * w_inv,
                _segment_sum(r * km * rk_w, ones_bd) * v)

    v, km, bv, cw, a_t, r_t, b_t, k_t, bonus = (
        jnp.concatenate(parts, axis=0) for parts in zip(*[prepare(p) for p in pairs]))

    chunks = range(len(pairs) * (tc // n))
    rows = [slice(c * n, (c + 1) * n) for c in chunks]
    cw_last = [cw[(c + 1) * n - 1:(c + 1) * n] for c in chunks]
    to_end = [jnp.exp(cw_last[c] - cw[rows[c]]) for c in chunks]
    bk_e = [bf(jnp.concatenate([bv[rows[c]] * to_end[c], km[rows[c]] * to_end[c]], axis=0))
            for c in chunks]
    zeros = jnp.zeros((n, LANES), F32)
    zeros_bd = jnp.zeros((2 * n, 2 * n), BF16)
    bd = lambda t: _block_diag(t, same_head)

    x = [_dot_nt(bf(jnp.concatenate([a_t[rows[c]], r_t[rows[c]]], axis=0)),
                 jnp.concatenate([bd(b_t[rows[c]]), bd(k_t[rows[c]])], axis=0)) for c in chunks]
    a_ab = [jnp.where(strict, x[c][:n, :LANES], 0.0) for c in chunks]
    a_ak = [jnp.where(strict, x[c][:n, LANES:], 0.0) for c in chunks]
    a_rbk = [bf(jnp.concatenate([jnp.where(incl, x[c][n:, :LANES], 0.0),
                                 jnp.where(incl, x[c][n:, LANES:], 0.0)], axis=1)) for c in chunks]

    inv = [jnp.where(diag, 1.0, a_ab[c]) for c in chunks]
    power = [_dot(bf(a_ab[c]), bd(a_ab[c])) for c in chunks]
    for _ in range(int(math.log2(n)) - 2):
        both = [_dot(bf(jnp.concatenate([inv[c], power[c]], axis=0)), bd(power[c])) for c in chunks]
        inv = [inv[c] + both[c][:n] for c in chunks]
        power = [both[c][n:] for c in chunks]
    inv = [inv[c] + _dot(bf(inv[c]), bd(power[c])) for c in chunks]

    av = [_dot(bf(a_ak[c]), bd(v[rows[c]])) for c in chunks]
    au = [_dot(bf(inv[c]), jnp.concatenate([bd(a_t[rows[c]]), bd(av[c])], axis=1))
          for c in chunks]
    gc = [_dot_tn(bk_e[c], bf(jnp.concatenate(
              [au[c], jnp.concatenate([zeros, v[rows[c]]], axis=1)], axis=0))) for c in chunks]
    g_mat = [_fold_diag(gc[c][:, :LANES], n, left) + jnp.where(diag, jnp.exp(cw_last[c]), 0.0)
             for c in chunks]
    c0 = [_fold_diag(gc[c][:, LANES:], n, left) for c in chunks]
    ry = [_dot(a_rbk[c], jnp.concatenate(
              [jnp.concatenate([bd(au[c][:, :LANES]), bd(au[c][:, LANES:])], axis=1),
               jnp.concatenate([zeros_bd, bd(v[rows[c]])], axis=1)], axis=0))
          for c in chunks]
    rg = [bf(jnp.concatenate([r_t[rows[c]] + ry[c][:, :LANES], g_mat[c]], axis=0)) for c in chunks]

    nc = tc // n
    st = [state_ref[p] for p in pairs]
    ys = [[] for _ in pairs]
    for c in range(nc):
        outs = [_dot(rg[p * nc + c], bd(st[p])) for p in pairs]
        for p in pairs:
            ys[p].append(outs[p][:n] + ry[p * nc + c][:, LANES:])
            st[p] = outs[p][n:] + c0[p * nc + c]
    for p in pairs:
        state_ref[p] = st[p]

    for p in pairs:
        lanes = slice(p * LANES, (p + 1) * LANES)
        y = jnp.concatenate(ys[p], axis=0)
        mu = _segment_sum(y, ones_bd) * (1.0 / hd)
        yc = y - mu
        var = _segment_sum(yc * yc, ones_bd) * (1.0 / hd)
        yn = yc * lax.rsqrt(var + RWKV_GN_EPS) * par[3:4, lanes] + par[4:5, lanes]
        o_ref[:, lanes] = ((yn + bonus[p * tc:(p + 1) * tc]) * g_ref[:, lanes].astype(F32)
                           ).astype(o_ref.dtype)


def _rwkv_core(rkv, ld, a, g, par, *, batch, seq, tc=512, pairs=4):
    m, d = ld.shape
    nt = seq // tc
    width = pairs * LANES
    blk = lambda b, hp, t: (b * nt + t, hp)
    sel = lambda which: pl.BlockSpec((None, tc, width), lambda b, hp, t: (which, b * nt + t, hp))
    return pl.pallas_call(
        functools.partial(_rwkv_core_kernel, tc=tc),
        grid=(batch, d // width, nt),
        in_specs=[sel(0), sel(1), sel(2),
                  pl.BlockSpec((tc, width), blk),
                  pl.BlockSpec((tc, width), blk),
                  pl.BlockSpec((tc, width), blk),
                  pl.BlockSpec((SUBLANES, width), lambda b, hp, t: (0, hp))],
        out_specs=pl.BlockSpec((tc, width), blk),
        out_shape=jax.ShapeDtypeStruct((m, d), BF16),
        scratch_shapes=[pltpu.VMEM((pairs, RWKV_HEAD_DIM, LANES), F32)],
        compiler_params=_params("parallel", "parallel", "arbitrary"),
    )(rkv, rkv, rkv, ld, a, g, par)


def _pad_cols(w, n):
    return jnp.pad(w, ((0, 0), (0, n - w.shape[1])))


def _pad_rows(w, n):
    return jnp.pad(w, ((0, n - w.shape[0]), (0, 0)))


def kernel(x, p, ln_gain, ln_bias, ret_w_in, ret_gn_gain, ret_w_out, swa_w_qkv, swa_sinks, swa_w_out, rel_bias, rwkv_mix, rwkv_w_rkv, rwkv_w0, rwkv_w1, rwkv_w2, rwkv_a0, rwkv_a1, rwkv_a2, rwkv_g1, rwkv_g2, rwkv_k_k, rwkv_k_a, rwkv_r_k, rwkv_gn_gain, rwkv_gn_bias, rwkv_w_out, ffn_w_up, ffn_conv_w, ffn_conv_b, ffn_w_down, ple_w_proj, ple_w_gate):
    batch, seq, d = x.shape
    m = batch * seq
    xf = x.reshape(m, d)
    xb = xf.astype(BF16)
    row = lambda v: v.reshape(1, -1)

    half = RET_QK_DIM // 2
    inv_freq = 1.0 / (ROPE_BASE ** (jnp.arange(0, RET_QK_DIM, 2, dtype=F32) / RET_QK_DIM))
    ang = jnp.arange(seq, dtype=F32)[:, None] * inv_freq[None, :]
    cos, sin = jnp.cos(ang), jnp.sin(ang)
    log_gamma = jnp.log(1.0 - 2.0 ** (-5.0 - jnp.arange(RET_HEADS, dtype=F32)))
    lg = jnp.broadcast_to(log_gamma[:, None, None], (RET_HEADS, 1, RET_V_DIM))

    for i in range(DEPTH):
        kind, j = i % N_MIXERS, i // N_MIXERS
        g1, b1 = row(ln_gain[i, 0]), row(ln_bias[i, 0])
        if kind == 0:
            proj = _ret_proj(xb, _col_tiles(ret_w_in[j].astype(BF16), 1024), cos, sin, seq=seq)
            o = _ret_core(proj, lg, row(ret_gn_gain[j]), batch=batch, seq=seq)
            xf, xb = _matmul_ln(o, ret_w_out[j].astype(BF16), xf, g1, b1)
        elif kind == 1:
            qkv = _matmul(xb, _col_tiles(swa_w_qkv[j].astype(BF16), 1280), tm=1024, out_dtype=BF16)
            o = _swa_core(qkv, _bias_table(rel_bias), swa_sinks[j], seq=seq)
            xf, xb = _matmul_ln(o, swa_w_out[j].astype(BF16), xf, g1, b1)
        else:
            mix = rwkv_mix[j]
            w_rkv = jnp.concatenate([_col_tiles(rwkv_w_rkv[j, t].astype(BF16), 1024) for t in range(3)])
            rkv = _rwkv_rkv(xf, mix[jnp.array([0, 2, 3])].reshape(3, 1, d), w_rkv, seq=seq)
            lora = RWKV_HEAD_DIM * 2
            ld, a, g = _rwkv_lora(
                xf, mix[jnp.array([1, 4, 5])], row(rwkv_w0[j]), row(rwkv_a0[j]),
                _pad_cols(rwkv_w1[j], lora).astype(BF16), _pad_rows(rwkv_w2[j], lora).astype(BF16),
                _pad_cols(rwkv_a1[j], lora).astype(BF16), _pad_rows(rwkv_a2[j], lora).astype(BF16),
                rwkv_g1[j].astype(BF16), rwkv_g2[j].astype(BF16), seq=seq)
            par = jnp.stack([rwkv_k_k[j], rwkv_k_a[j], rwkv_r_k[j].reshape(-1), rwkv_gn_gain[j],
                             rwkv_gn_bias[j]] + [jnp.zeros((d,), F32)] * 3)
            o = _rwkv_core(rkv, ld, a, g, par, batch=batch, seq=seq)
            xf, xb = _matmul_ln(o, rwkv_w_out[j].astype(BF16), xf, g1, b1)

        w_up = ffn_w_up[i].astype(BF16)
        conv = jnp.concatenate([ffn_conv_w[i], ffn_conv_b[i][None]], axis=0)
        conv = jnp.pad(conv, ((0, SUBLANES - conv.shape[0]), (0, 0)))
        up_gate = lambda t: jnp.concatenate(
            [_col_tiles(_pad_cols(t[:, :D_FF], D_FF_PAD), FFN_COL_TILE),
             _col_tiles(_pad_cols(t[:, D_FF:], D_FF_PAD), FFN_COL_TILE)], axis=2)
        xf, xb = _ffn(xb, xf, up_gate(w_up), up_gate(conv),
                      _pad_rows(ffn_w_down[i], D_FF_PAD).astype(BF16),
                      row(ln_gain[i, 1]), row(ln_bias[i, 1]), seq=seq)
        xf, xb = _ple(xb, xf, p[i].reshape(m, PLE_DIM),
                      _col_tiles(ple_w_gate[i].astype(BF16), 1024),
                      _col_tiles(ple_w_proj[i].astype(BF16), 1024))
    return xf.reshape(batch, seq, d)
```

```python
import functools
import math

import numpy as np
import jax
import jax.numpy as jnp
from jax import lax
from jax.experimental import pallas as pl
from jax.experimental.pallas import tpu as pltpu

F32 = jnp.float32
BF16 = jnp.bfloat16

D_MODEL = 2048
DEPTH = 4
N_MIXERS = 3

RET_HEADS = 8
RET_QK_DIM = D_MODEL // RET_HEADS
RET_V_DIM = 2 * D_MODEL // RET_HEADS
RET_CHUNK = 128
RET_GN_EPS = 1e-5
ROPE_BASE = 10000.0

SWA_HEAD_DIM = 64
SWA_Q_HEADS = D_MODEL // SWA_HEAD_DIM
SWA_KV_HEADS = SWA_Q_HEADS // 8
SWA_GROUP = SWA_Q_HEADS // SWA_KV_HEADS
SWA_WINDOW = 128
REL_BUCKETS = 32
REL_MAX_DIST = SWA_WINDOW
NEG_INF = -1e30

RWKV_HEAD_DIM = 64
RWKV_HEADS = D_MODEL // RWKV_HEAD_DIM
RWKV_GN_EPS = 64e-5
RWKV_CHUNK = 64

D_FF = 5504
CONV_WIDTH = 3
PLE_DIM = 256

LN_EPS = 1e-5
DEEPNORM_ALPHA = (2.0 * DEPTH) ** 0.25

LANES = 128
SUBLANES = 8
BF16_ROWS = 16
VMEM_LIMIT_BYTES = 56 * 1024 * 1024
FFN_COL_TILE = 512
D_FF_PAD = -(-D_FF // FFN_COL_TILE) * FFN_COL_TILE


def _params(*semantics, flags=None):
    return pltpu.CompilerParams(dimension_semantics=semantics,
                                vmem_limit_bytes=VMEM_LIMIT_BYTES, flags=flags)


def _dot(a, b):
    return jnp.dot(a, b, preferred_element_type=F32)


def _dot_nt(a, b):
    return lax.dot_general(a, b, (((1,), (1,)), ((), ())), preferred_element_type=F32)


def _dot_tn(a, b):
    return lax.dot_general(a, b, (((0,), (0,)), ((), ())), preferred_element_type=F32)


def _layer_norm_rows(z, gain, bias):
    mu = jnp.mean(z, axis=-1, keepdims=True)
    zc = z - mu
    var = jnp.mean(zc * zc, axis=-1, keepdims=True)
    return zc * lax.rsqrt(var + LN_EPS) * gain + bias


def _sigmoid(x):
    return 1.0 / (1.0 + jnp.exp(-x))


def _silu(x):
    return x * _sigmoid(x)


def _matmul_kernel(a_ref, w_ref, o_ref):
    o_ref[...] = _dot(a_ref[...], w_ref[...]).astype(o_ref.dtype)


def _matmul(a, w, *, tm, tn, out_dtype):
    m, k = a.shape
    n = w.shape[1]
    return pl.pallas_call(
        _matmul_kernel,
        grid=(m // tm, n // tn),
        in_specs=[pl.BlockSpec((tm, k), lambda i, j: (i, 0)),
                  pl.BlockSpec((k, tn), lambda i, j: (0, j))],
        out_specs=pl.BlockSpec((tm, tn), lambda i, j: (i, j)),
        out_shape=jax.ShapeDtypeStruct((m, n), out_dtype),
        compiler_params=_params("parallel", "arbitrary"),
    )(a, w)


def _matmul_ln_kernel(a_ref, w_ref, res_ref, gain_ref, bias_ref, xo_ref, xb_ref):
    y = _dot(a_ref[...], w_ref[...])
    xn = _layer_norm_rows(DEEPNORM_ALPHA * res_ref[...] + y, gain_ref[...], bias_ref[...])
    xo_ref[...] = xn
    xb_ref[...] = xn.astype(BF16)


def _matmul_ln(a, w, res, gain, bias, *, tm):
    m, k = a.shape
    d = w.shape[1]
    row = lambda i: (i, 0)
    const = lambda i: (0, 0)
    return pl.pallas_call(
        _matmul_ln_kernel,
        grid=(m // tm,),
        in_specs=[pl.BlockSpec((tm, k), row),
                  pl.BlockSpec((k, d), const),
                  pl.BlockSpec((tm, d), row),
                  pl.BlockSpec((1, d), const),
                  pl.BlockSpec((1, d), const)],
        out_specs=[pl.BlockSpec((tm, d), row), pl.BlockSpec((tm, d), row)],
        out_shape=[jax.ShapeDtypeStruct((m, d), F32), jax.ShapeDtypeStruct((m, d), BF16)],
        compiler_params=_params("parallel"),
    )(a, w, res, gain, bias)


def _ffn_kernel(xb_ref, halo_ref, res_ref, wu_ref, wg_ref, cu_ref, cg_ref, wd_ref,
                gain_ref, bias_ref, xo_ref, xob_ref, acc_ref, *maybe_x_ref,
                tm, tiles_per_seq, nf, tf, sub, row_split):
    i = pl.program_id(0)
    f = pl.program_id(1)

    def extended_rows():
        halo = jnp.where(i % tiles_per_seq == 0, jnp.zeros_like(halo_ref[...]), halo_ref[...])
        return jnp.concatenate([halo, xb_ref[...]], axis=0)

    if maybe_x_ref:
        x, = maybe_x_ref

        @pl.when(f == 0)
        def _():
            x[...] = extended_rows()
    else:
        x = extended_rows()

    def conv_branch(xr, w_ref, c_ref, cols):
        h = _dot(xr, w_ref[:, cols])
        c = c_ref[:, cols]
        out = c[3:4] + h[BF16_ROWS:] * c[2:3]
        out = out + pltpu.roll(h, 1, axis=0)[BF16_ROWS:] * c[1:2]
        out = out + pltpu.roll(h, 2, axis=0)[BF16_ROWS:] * c[0:1]
        return out

    @pl.when(f == 0)
    def _():
        acc_ref[...] = jnp.zeros_like(acc_ref)

    rows = tm // row_split
    for rb in range(row_split):
        xr = x[rb * rows:(rb + 1) * rows + BF16_ROWS]
        acts = []
        for s in range(tf // sub):
            cols = slice(s * sub, (s + 1) * sub)
            u = conv_branch(xr, wu_ref, cu_ref, cols)
            gate = conv_branch(xr, wg_ref, cg_ref, cols)
            acts.append((_silu(gate) * u).astype(BF16))
        acc_ref[rb * rows:(rb + 1) * rows, :] += _dot(jnp.concatenate(acts, axis=1), wd_ref[...])

    @pl.when(f == nf - 1)
    def _():
        xn = _layer_norm_rows(DEEPNORM_ALPHA * res_ref[...] + acc_ref[...], gain_ref[...],
                              bias_ref[...])
        xo_ref[...] = xn
        xob_ref[...] = xn.astype(BF16)


def _ffn(xb, xres, w_ug, c_ug, wd, gain, bias, *, seq, tm=512, tf=FFN_COL_TILE, sub=256,
         row_split=1, hoist_x=False):
    m, d = xb.shape
    nf = w_ug.shape[2] // tf
    halo_blocks = tm // BF16_ROWS
    row = lambda i, f: (i, 0)
    const = lambda i, f: (0, 0)
    half = lambda rows, which: pl.BlockSpec((None, rows, tf), lambda i, f: (which, 0, f))
    scratch = [pltpu.VMEM((tm, d), F32)]
    if hoist_x:
        scratch.append(pltpu.VMEM((BF16_ROWS + tm, d), BF16))
    return pl.pallas_call(
        functools.partial(_ffn_kernel, tm=tm, tiles_per_seq=seq // tm, nf=nf, tf=tf, sub=sub,
                          row_split=row_split),
        grid=(m // tm, nf),
        in_specs=[pl.BlockSpec((tm, d), row),
                  pl.BlockSpec((BF16_ROWS, d), lambda i, f: (jnp.maximum(i * halo_blocks - 1, 0), 0)),
                  pl.BlockSpec((tm, d), row),
                  half(d, 0), half(d, 1), half(SUBLANES, 0), half(SUBLANES, 1),
                  pl.BlockSpec((tf, d), lambda i, f: (f, 0)),
                  pl.BlockSpec((1, d), const),
                  pl.BlockSpec((1, d), const)],
        out_specs=[pl.BlockSpec((tm, d), row), pl.BlockSpec((tm, d), row)],
        out_shape=[jax.ShapeDtypeStruct((m, d), F32), jax.ShapeDtypeStruct((m, d), BF16)],
        scratch_shapes=scratch,
        compiler_params=_params("parallel", "arbitrary"),
    )(xb, xb, xres, w_ug, w_ug, c_ug, c_ug, wd, gain, bias)


def _ple_kernel(xb_ref, wg_ref, p_ref, wp_ref, res_ref, xo_ref, xob_ref):
    gate = _sigmoid(_dot(xb_ref[...], wg_ref[...]))
    proj = _dot(p_ref[...].astype(BF16), wp_ref[...])
    xn = res_ref[...] + proj * gate
    xo_ref[...] = xn
    xob_ref[...] = xn.astype(BF16)


def _ple(xb, xres, p, wg, wp, *, tm=512, tn=2048):
    m, d = xb.shape
    pd = p.shape[1]
    return pl.pallas_call(
        _ple_kernel,
        grid=(m // tm, d // tn),
        in_specs=[pl.BlockSpec((tm, d), lambda i, j: (i, 0)),
                  pl.BlockSpec((d, tn), lambda i, j: (0, j)),
                  pl.BlockSpec((tm, pd), lambda i, j: (i, 0)),
                  pl.BlockSpec((pd, tn), lambda i, j: (0, j)),
                  pl.BlockSpec((tm, tn), lambda i, j: (i, j))],
        out_specs=[pl.BlockSpec((tm, tn), lambda i, j: (i, j)),
                   pl.BlockSpec((tm, tn), lambda i, j: (i, j))],
        out_shape=[jax.ShapeDtypeStruct((m, d), F32), jax.ShapeDtypeStruct((m, d), BF16)],
        compiler_params=_params("parallel", "arbitrary"),
    )(xb, wg, p, wp, xres)


def _ret_proj_kernel(a_ref, w_ref, cos_ref, sin_ref, o_ref, *, tn, nq, nkv):
    j = pl.program_id(1)
    acc = _dot(a_ref[...], w_ref[...])
    half = RET_QK_DIM // 2

    def rotary(scale):
        cos = cos_ref[...]
        sin = sin_ref[...]
        for h in range(tn // RET_QK_DIM):
            lo = h * RET_QK_DIM
            x1 = acc[:, lo:lo + half]
            x2 = acc[:, lo + half:lo + RET_QK_DIM]
            o_ref[:, lo:lo + half] = ((x1 * cos - x2 * sin) * scale).astype(o_ref.dtype)
            o_ref[:, lo + half:lo + RET_QK_DIM] = ((x1 * sin + x2 * cos) * scale).astype(o_ref.dtype)

    @pl.when(j < nq)
    def _():
        rotary(1.0)

    @pl.when((j >= nq) & (j < 2 * nq))
    def _():
        rotary(RET_QK_DIM ** -0.5)

    @pl.when((j >= 2 * nq) & (j < 2 * nq + nkv))
    def _():
        o_ref[...] = acc.astype(o_ref.dtype)

    @pl.when(j >= 2 * nq + nkv)
    def _():
        o_ref[...] = _silu(acc).astype(o_ref.dtype)


def _ret_proj(xb, w, cos, sin, *, seq, tm=1024, tn=2048):
    m, d = xb.shape
    n = w.shape[1]
    nq = RET_HEADS * RET_QK_DIM // tn
    nkv = RET_HEADS * RET_V_DIM // tn
    tiles_per_seq = seq // tm
    return pl.pallas_call(
        functools.partial(_ret_proj_kernel, tn=tn, nq=nq, nkv=nkv),
        grid=(m // tm, n // tn),
        in_specs=[pl.BlockSpec((tm, d), lambda i, j: (i, 0)),
                  pl.BlockSpec((d, tn), lambda i, j: (0, j)),
                  pl.BlockSpec((tm, RET_QK_DIM // 2), lambda i, j: (i % tiles_per_seq, 0)),
                  pl.BlockSpec((tm, RET_QK_DIM // 2), lambda i, j: (i % tiles_per_seq, 0))],
        out_specs=pl.BlockSpec((tm, tn), lambda i, j: (i, j)),
        out_shape=jax.ShapeDtypeStruct((m, n), BF16),
        compiler_params=_params("parallel", "arbitrary"),
    )(xb, w, cos, sin)


def _ret_core_kernel(q_ref, k_ref, v_ref, g_ref, lg_ref, gain_ref, o_ref, state_ref, *, ts):
    c = RET_CHUNK

    @pl.when(pl.program_id(2) == 0)
    def _():
        state_ref[...] = jnp.zeros_like(state_ref)

    lg = lg_ref[0]
    ii = lax.broadcasted_iota(jnp.int32, (c, c), 0)
    jj = lax.broadcasted_iota(jnp.int32, (c, c), 1)
    diff = (ii - jj).astype(F32)
    decay_mask = jnp.where(ii >= jj, jnp.exp(lg[:, :c] * jnp.maximum(diff, 0.0)), 0.0)
    row_v = lax.broadcasted_iota(jnp.int32, (c, RET_V_DIM), 0).astype(F32)
    row_k = lax.broadcasted_iota(jnp.int32, (c, RET_QK_DIM), 0).astype(F32)
    q_decay = jnp.exp(lg * (row_v + 1.0))
    k_decay = jnp.exp(lg[:, :RET_QK_DIM] * (c - 1.0 - row_k))
    chunk_decay = jnp.exp(lg * float(c))
    gain = gain_ref[...]

    chunks = range(ts // c)
    rows = [slice(n * c, (n + 1) * c) for n in chunks]
    scores = [(_dot_nt(q_ref[rows[n], :], k_ref[rows[n], :]) * decay_mask).astype(BF16)
              for n in chunks]
    inner = [_dot(scores[n], v_ref[rows[n], :]) for n in chunks]
    kv = [_dot_tn((k_ref[rows[n], :].astype(F32) * k_decay).astype(BF16), v_ref[rows[n], :])
          for n in chunks]
    state = state_ref[...]
    states = []
    for n in chunks:
        states.append(state.astype(BF16))
        state = state * chunk_decay + kv[n]
    state_ref[...] = state
    cross = [_dot(q_ref[rows[n], :], states[n]) * q_decay for n in chunks]
    for n in chunks:
        o = inner[n] + cross[n]
        mu = jnp.mean(o, axis=-1, keepdims=True)
        oc = o - mu
        var = jnp.mean(oc * oc, axis=-1, keepdims=True)
        on = oc * lax.rsqrt(var + RET_GN_EPS) * gain
        o_ref[rows[n], :] = (g_ref[rows[n], :].astype(F32) * on).astype(o_ref.dtype)


def _ret_core(proj, lg, gn_gain, *, batch, seq, ts=1024):
    m = proj.shape[0]
    nt = seq // ts
    kq = RET_HEADS
    row = lambda b, h, t: b * nt + t
    return pl.pallas_call(
        functools.partial(_ret_core_kernel, ts=ts),
        grid=(batch, RET_HEADS, nt),
        in_specs=[pl.BlockSpec((ts, RET_QK_DIM), lambda b, h, t: (row(b, h, t), h)),
                  pl.BlockSpec((ts, RET_QK_DIM), lambda b, h, t: (row(b, h, t), kq + h)),
                  pl.BlockSpec((ts, RET_V_DIM), lambda b, h, t: (row(b, h, t), kq + h)),
                  pl.BlockSpec((ts, RET_V_DIM), lambda b, h, t: (row(b, h, t), 2 * kq + h)),
                  pl.BlockSpec((1, 1, RET_V_DIM), lambda b, h, t: (h, 0, 0)),
                  pl.BlockSpec((1, RET_V_DIM), lambda b, h, t: (0, h))],
        out_specs=pl.BlockSpec((ts, RET_V_DIM), lambda b, h, t: (row(b, h, t), h)),
        out_shape=jax.ShapeDtypeStruct((m, RET_HEADS * RET_V_DIM), BF16),
        scratch_shapes=[pltpu.VMEM((RET_QK_DIM, RET_V_DIM), F32)],
        compiler_params=_params("parallel", "parallel", "arbitrary"),
    )(proj, proj, proj, proj, lg, gn_gain)


def _t5_buckets(dist):
    n = np.maximum(dist, 0)
    max_exact = REL_BUCKETS // 2
    large = max_exact + (np.log(np.maximum(n, 1) / max_exact) / np.log(REL_MAX_DIST / max_exact)
                         * (REL_BUCKETS - max_exact)).astype(np.int32)
    large = np.minimum(large, REL_BUCKETS - 1)
    return np.where(n < max_exact, n, large).astype(np.int32)


def _bias_table_kernel(rel_ref, bucket_ref, o_ref):
    h = pl.program_id(0)
    buckets = bucket_ref[...]
    table = jnp.zeros(buckets.shape, F32)
    for b in range(REL_BUCKETS):
        table = jnp.where(buckets == b, rel_ref[b, h], table)
    o_ref[0] = table


def _bias_table(rel_bias):
    c = SWA_WINDOW
    dist = np.arange(c)[:, None] + c - np.arange(2 * c)[None, :]
    buckets = jnp.asarray(_t5_buckets(dist))
    return pl.pallas_call(
        _bias_table_kernel,
        grid=(SWA_Q_HEADS,),
        in_specs=[pl.BlockSpec(memory_space=pltpu.SMEM),
                  pl.BlockSpec((c, 2 * c), lambda h: (0, 0))],
        out_specs=pl.BlockSpec((1, c, 2 * c), lambda h: (h, 0, 0)),
        out_shape=jax.ShapeDtypeStruct((SWA_Q_HEADS, c, 2 * c), F32),
        compiler_params=_params("arbitrary"),
    )(rel_bias, buckets)


def _swa_kernel(q_ref, kp_ref, kc_ref, vp_ref, vc_ref, bias_ref, sink_ref, o_ref, *, blocks_per_seq):
    c = SWA_WINDOW
    hd = SWA_HEAD_DIM
    has_prev = (pl.program_id(0) % blocks_per_seq) > 0
    ii = lax.broadcasted_iota(jnp.int32, (c, 2 * c), 0)
    jj = lax.broadcasted_iota(jnp.int32, (c, 2 * c), 1)
    ok = ((jj < c) & (jj > ii) & has_prev) | ((jj >= c) & (jj - c <= ii))
    left = lax.broadcasted_iota(jnp.int32, (c, LANES), 1) < hd
    ones = jnp.ones((2 * c, LANES), BF16)
    zeros = jnp.zeros((2 * c, hd), BF16)
    scale = hd ** -0.5
    for g in range(SWA_KV_HEADS):
        cols = slice(g * hd, (g + 1) * hd)
        k = jnp.concatenate([kp_ref[:, cols], kc_ref[:, cols]], axis=0)
        v = jnp.concatenate([vp_ref[:, cols], vc_ref[:, cols]], axis=0)
        k_half = (jnp.concatenate([k, zeros], axis=1), jnp.concatenate([zeros, k], axis=1))
        v_half = (jnp.concatenate([v, zeros], axis=1), jnp.concatenate([zeros, v], axis=1))
        heads = range(g * SWA_GROUP, (g + 1) * SWA_GROUP)
        s = [_dot_nt(q_ref[:, (h // 2) * LANES:(h // 2 + 1) * LANES], k_half[h % 2]) for h in heads]
        s = [jnp.where(ok, s[i] * scale + bias_ref[h], NEG_INF) for i, h in enumerate(heads)]
        mx = [jnp.maximum(jnp.max(s[i], axis=-1, keepdims=True), sink_ref[h])
              for i, h in enumerate(heads)]
        p = [jnp.exp(s[i] - mx[i]).astype(BF16) for i in range(SWA_GROUP)]
        den = [_dot(p[i], ones) + jnp.exp(sink_ref[h] - mx[i]) for i, h in enumerate(heads)]
        for i in range(0, SWA_GROUP, 2):
            num = _dot(p[i], v_half[0]) + _dot(p[i + 1], v_half[1])
            out = num / jnp.where(left, den[i], den[i + 1])
            lo = (g * SWA_GROUP + i) * hd
            o_ref[:, lo:lo + LANES] = out.astype(o_ref.dtype)


def _swa_core(qkv, bias, sinks, *, seq):
    m = qkv.shape[0]
    c = SWA_WINDOW
    blocks_per_seq = seq // c
    qw = SWA_Q_HEADS * SWA_HEAD_DIM
    kw = SWA_KV_HEADS * SWA_HEAD_DIM
    kcol = qw // kw
    prev = lambda r: jnp.where(r % blocks_per_seq == 0, r, r - 1)
    return pl.pallas_call(
        functools.partial(_swa_kernel, blocks_per_seq=blocks_per_seq),
        grid=(m // c,),
        in_specs=[pl.BlockSpec((c, qw), lambda r: (r, 0)),
                  pl.BlockSpec((c, kw), lambda r: (prev(r), kcol)),
                  pl.BlockSpec((c, kw), lambda r: (r, kcol)),
                  pl.BlockSpec((c, kw), lambda r: (prev(r), kcol + 1)),
                  pl.BlockSpec((c, kw), lambda r: (r, kcol + 1)),
                  pl.BlockSpec((SWA_Q_HEADS, c, 2 * c), lambda r: (0, 0, 0)),
                  pl.BlockSpec(memory_space=pltpu.SMEM)],
        out_specs=pl.BlockSpec((c, qw), lambda r: (r, 0)),
        out_shape=jax.ShapeDtypeStruct((m, qw), BF16),
        compiler_params=_params("parallel"),
    )(qkv, qkv, qkv, qkv, qkv, bias, sinks)


def _shifted_delta(x, halo_ref, at_seq_start):
    first = jnp.where(at_seq_start, jnp.zeros_like(halo_ref[SUBLANES - 1:, :]), halo_ref[SUBLANES - 1:, :])
    rows = lax.broadcasted_iota(jnp.int32, x.shape, 0)
    prev = jnp.where(rows == 0, first, pltpu.roll(x, 1, axis=0))
    return prev - x


def _rwkv_rkv_kernel(x_ref, halo_ref, mix_ref, w_ref, o_ref, amix_ref, *, tiles_per_seq, nj):
    i = pl.program_id(0)
    j = pl.program_id(1)

    @pl.when(j % nj == 0)
    def _():
        x = x_ref[...]
        xx = _shifted_delta(x, halo_ref, i % tiles_per_seq == 0)
        amix_ref[...] = (x + xx * mix_ref[0]).astype(BF16)

    o_ref[...] = _dot(amix_ref[...], w_ref[...])


def _rwkv_rkv(x, mix3, w3, *, seq, tm=512, tn=2048):
    m, d = x.shape
    nj = d // tn
    halo_blocks = tm // SUBLANES
    return pl.pallas_call(
        functools.partial(_rwkv_rkv_kernel, tiles_per_seq=seq // tm, nj=nj),
        grid=(m // tm, 3 * nj),
        in_specs=[pl.BlockSpec((tm, d), lambda i, j: (i, 0)),
                  pl.BlockSpec((SUBLANES, d), lambda i, j: (jnp.maximum(i * halo_blocks - 1, 0), 0)),
                  pl.BlockSpec((1, 1, d), lambda i, j: (j // nj, 0, 0)),
                  pl.BlockSpec((None, d, tn), lambda i, j: (j // nj, 0, j % nj))],
        out_specs=pl.BlockSpec((None, tm, tn), lambda i, j: (j // nj, i, j % nj)),
        out_shape=jax.ShapeDtypeStruct((3, m, d), F32),
        scratch_shapes=[pltpu.VMEM((tm, d), BF16)],
        compiler_params=_params("parallel", "arbitrary"),
    )(x, x, mix3, w3)


def _rwkv_lora_kernel(x_ref, halo_ref, mix_ref, w0_ref, a0_ref, w1_ref, w2_ref, a1_ref, a2_ref,
                      g1_ref, g2_ref, ld_ref, a_ref, g_ref, *, tiles_per_seq):
    i = pl.program_id(0)
    x = x_ref[...]
    xx = _shifted_delta(x, halo_ref, i % tiles_per_seq == 0)
    mix = mix_ref[...]
    xw = (x + xx * mix[0:1]).astype(BF16)
    xa = (x + xx * mix[1:2]).astype(BF16)
    xg = (x + xx * mix[2:3]).astype(BF16)
    z = w0_ref[...] + _dot(jnp.tanh(_dot(xw, w1_ref[...])).astype(BF16), w2_ref[...])
    ld_ref[...] = -math.exp(-0.5) * _sigmoid(z)
    a_ref[...] = _sigmoid(a0_ref[...] + _dot(_dot(xa, a1_ref[...]).astype(BF16), a2_ref[...]))
    g_ref[...] = _dot(_sigmoid(_dot(xg, g1_ref[...])).astype(BF16), g2_ref[...]).astype(g_ref.dtype)


def _rwkv_lora(x, mix3, w0, a0, w1, w2, a1, a2, g1, g2, *, seq, tm=512):
    m, d = x.shape
    halo_blocks = tm // SUBLANES
    row = lambda i: (i, 0)
    full = lambda a: pl.BlockSpec(a.shape, lambda i: (0, 0))
    return pl.pallas_call(
        functools.partial(_rwkv_lora_kernel, tiles_per_seq=seq // tm),
        grid=(m // tm,),
        in_specs=[pl.BlockSpec((tm, d), row),
                  pl.BlockSpec((SUBLANES, d), lambda i: (jnp.maximum(i * halo_blocks - 1, 0), 0)),
                  full(mix3), full(w0), full(a0), full(w1), full(w2), full(a1), full(a2),
                  full(g1), full(g2)],
        out_specs=[pl.BlockSpec((tm, d), row)] * 3,
        out_shape=[jax.ShapeDtypeStruct((m, d), F32), jax.ShapeDtypeStruct((m, d), F32),
                   jax.ShapeDtypeStruct((m, d), BF16)],
        compiler_params=_params("parallel"),
    )(x, x, mix3, w0, a0, w1, w2, a1, a2, g1, g2)


def _pair_mask(n):
    rows = lax.broadcasted_iota(jnp.int32, (2 * n, 2 * n), 0)
    cols = lax.broadcasted_iota(jnp.int32, (2 * n, 2 * n), 1)
    return (rows < n) == (cols < n)


def _block_diag(y, same_head):
    return jnp.where(same_head, jnp.concatenate([y, y], axis=0), 0.0).astype(BF16)


def _fold_diag(x, n, left):
    return jnp.where(left, x[:n], x[n:])


def _segment_sum(x, ones_bd):
    hi = x.astype(BF16)
    lo = (x - hi.astype(F32)).astype(BF16)
    return _dot(hi, ones_bd) + _dot(lo, ones_bd)


def _chunk_cumsum(x, n):
    pos = lax.broadcasted_iota(jnp.int32, x.shape, 0) % n
    shift = 1
    while shift < n:
        x = x + jnp.where(pos >= shift, pltpu.roll(x, shift, axis=0), 0.0)
        shift *= 2
    return x


def _rwkv_core_kernel(r_ref, k_ref, v_ref, ld_ref, a_ref, g_ref, par_ref, o_ref, state_ref, *, tc):
    hd = RWKV_HEAD_DIM
    n = RWKV_CHUNK
    assert n == hd and 2 * hd == LANES
    bf = lambda t: t.astype(BF16)

    @pl.when(pl.program_id(2) == 0)
    def _():
        state_ref[...] = jnp.zeros_like(state_ref)

    same_head = _pair_mask(n)
    ones_bd = jnp.where(same_head, 1.0, 0.0).astype(BF16)
    lane = lax.broadcasted_iota(jnp.int32, (n, LANES), 1)
    row = lax.broadcasted_iota(jnp.int32, (n, LANES), 0)
    left = lane < hd
    pos = lane % hd
    strict = row > pos
    incl = row >= pos
    diag = row == pos

    pairs = range(r_ref.shape[1] // LANES)
    par = par_ref[...]

    def prepare(p):
        lanes = slice(p * LANES, (p + 1) * LANES)
        kk_w, ka_w, rk_w = (par[i:i + 1, lanes] for i in range(3))
        r, k, v, ld, a = (ref[:, lanes] for ref in (r_ref, k_ref, v_ref, ld_ref, a_ref))
        kk = k * kk_w
        kk = kk / jnp.maximum(jnp.sqrt(_segment_sum(kk * kk, ones_bd)), 1e-12)
        km = k * (1.0 + (a - 1.0) * ka_w)
        bv = kk * a
        cw = _chunk_cumsum(ld, n)
        w_inv = jnp.exp(-cw)
        return (v, km, bv, cw, -kk * jnp.exp(cw - ld), r * jnp.exp(cw), bv * w_inv, km * w_inv,
                _segment_sum(r * km * rk_w, ones_bd) * v)

    v, km, bv, cw, a_t, r_t, b_t, k_t, bonus = (
        jnp.concatenate(parts, axis=0) for parts in zip(*[prepare(p) for p in pairs]))

    chunks = range(len(pairs) * (tc // n))
    rows = [slice(c * n, (c + 1) * n) for c in chunks]
    cw_last = [cw[(c + 1) * n - 1:(c + 1) * n] for c in chunks]
    to_end = [jnp.exp(cw_last[c] - cw[rows[c]]) for c in chunks]
    bk_e = [bf(jnp.concatenate([bv[rows[c]] * to_end[c], km[rows[c]] * to_end[c]], axis=0))
            for c in chunks]
    zeros = jnp.zeros((n, LANES), F32)
    zeros_bd = jnp.zeros((2 * n, 2 * n), BF16)
    bd = lambda t: _block_diag(t, same_head)

    x = [_dot_nt(bf(jnp.concatenate([a_t[rows[c]], r_t[rows[c]]], axis=0)),
                 jnp.concatenate([bd(b_t[rows[c]]), bd(k_t[rows[c]])], axis=0)) for c in chunks]
    a_ab = [jnp.where(strict, x[c][:n, :LANES], 0.0) for c in chunks]
    a_ak = [jnp.where(strict, x[c][:n, LANES:], 0.0) for c in chunks]
    a_rbk = [bf(jnp.concatenate([jnp.where(incl, x[c][n:, :LANES], 0.0),
                                 jnp.where(incl, x[c][n:, LANES:], 0.0)], axis=1)) for c in chunks]

    inv = [jnp.where(diag, 1.0, a_ab[c]) for c in chunks]
    power = [_dot(bf(a_ab[c]), bd(a_ab[c])) for c in chunks]
    for _ in range(int(math.log2(n)) - 2):
        both = [_dot(bf(jnp.concatenate([inv[c], power[c]], axis=0)), bd(power[c])) for c in chunks]
        inv = [inv[c] + both[c][:n] for c in chunks]
        power = [both[c][n:] for c in chunks]
    inv = [inv[c] + _dot(bf(inv[c]), bd(power[c])) for c in chunks]

    av = [_dot(bf(a_ak[c]), bd(v[rows[c]])) for c in chunks]
    au = [_dot(bf(inv[c]), jnp.concatenate([bd(a_t[rows[c]]), bd(av[c])], axis=1))
          for c in chunks]
    gc = [_dot_tn(bk_e[c], bf(jnp.concatenate(
              [au[c], jnp.concatenate([zeros, v[rows[c]]], axis=1)], axis=0))) for c in chunks]
    g_mat = [_fold_diag(gc[c][:, :LANES], n, left) + jnp.where(diag, jnp.exp(cw_last[c]), 0.0)
             for c in chunks]
    c0 = [_fold_diag(gc[c][:, LANES:], n, left) for c in chunks]
    ry = [_dot(a_rbk[c], jnp.concatenate(
              [jnp.concatenate([bd(au[c][:, :LANES]), bd(au[c][:, LANES:])], axis=1),
               jnp.concatenate([zeros_bd, bd(v[rows[c]])], axis=1)], axis=0))
          for c in chunks]
    rg = [bf(jnp.concatenate([r_t[rows[c]] + ry[c][:, :LANES], g_mat[c]], axis=0)) for c in chunks]

    nc = tc // n
    st = [state_ref[p] for p in pairs]
    ys = [[] for _ in pairs]
    for c in range(nc):
        outs = [_dot(rg[p * nc + c], bd(st[p])) for p in pairs]
        for p in pairs:
            ys[p].append(outs[p][:n] + ry[p * nc + c][:, LANES:])
            st[p] = outs[p][n:] + c0[p * nc + c]
    for p in pairs:
        state_ref[p] = st[p]

    for p in pairs:
        lanes = slice(p * LANES, (p + 1) * LANES)
        y = jnp.concatenate(ys[p], axis=0)
        mu = _segment_sum(y, ones_bd) * (1.0 / hd)
        yc = y - mu
        var = _segment_sum(yc * yc, ones_bd) * (1.0 / hd)
        yn = yc * lax.rsqrt(var + RWKV_GN_EPS) * par[3:4, lanes] + par[4:5, lanes]
        o_ref[:, lanes] = ((yn + bonus[p * tc:(p + 1) * tc]) * g_ref[:, lanes].astype(F32)
                           ).astype(o_ref.dtype)


def _rwkv_core(rkv, ld, a, g, par, *, batch, seq, tc=512, pairs=4):
    m, d = ld.shape
    nt = seq // tc
    width = pairs * LANES
    blk = lambda b, hp, t: (b * nt + t, hp)
    sel = lambda which: pl.BlockSpec((None, tc, width), lambda b, hp, t: (which, b * nt + t, hp))
    return pl.pallas_call(
        functools.partial(_rwkv_core_kernel, tc=tc),
        grid=(batch, d // width, nt),
        in_specs=[sel(0), sel(1), sel(2),
                  pl.BlockSpec((tc, width), blk),
                  pl.BlockSpec((tc, width), blk),
                  pl.BlockSpec((tc, width), blk),
                  pl.BlockSpec((SUBLANES, width), lambda b, hp, t: (0, hp))],
        out_specs=pl.BlockSpec((tc, width), blk),
        out_shape=jax.ShapeDtypeStruct((m, d), BF16),
        scratch_shapes=[pltpu.VMEM((pairs, RWKV_HEAD_DIM, LANES), F32)],
        compiler_params=_params("parallel", "parallel", "arbitrary"),
    )(rkv, rkv, rkv, ld, a, g, par)


CAST_BLOCK_BYTES = 8 * 1024 * 1024


def _cast_rows(k, n):
    tr = k
    while tr * n * 4 > CAST_BLOCK_BYTES and tr % 2 == 0 and (tr // 2) % BF16_ROWS == 0:
        tr //= 2
    return tr


def _cast_kernel(x_ref, o_ref):
    o_ref[...] = x_ref[...].astype(o_ref.dtype)


def _to_bf16(w, layer):
    _, k, n = w.shape
    tr = _cast_rows(k, n)
    return pl.pallas_call(
        _cast_kernel,
        grid=(k // tr,),
        in_specs=[pl.BlockSpec((None, tr, n), lambda r: (layer, r, 0))],
        out_specs=pl.BlockSpec((tr, n), lambda r: (r, 0)),
        out_shape=jax.ShapeDtypeStruct((k, n), BF16),
        compiler_params=_params("parallel"),
    )(w)


def _all_to_bf16(w):
    layers, k, n = w.shape
    tr = _cast_rows(k, n)
    return pl.pallas_call(
        _cast_kernel,
        grid=(layers, k // tr),
        in_specs=[pl.BlockSpec((None, tr, n), lambda l, r: (l, r, 0))],
        out_specs=pl.BlockSpec((None, tr, n), lambda l, r: (l, r, 0)),
        out_shape=jax.ShapeDtypeStruct((layers, k, n), BF16),
        compiler_params=_params("parallel", "parallel"),
    )(w)


def _ffn_up_cast_kernel(x_ref, o_ref):
    o_ref[0, :, :D_FF] = x_ref[:, :D_FF].astype(o_ref.dtype)
    o_ref[1, :, :D_FF] = x_ref[:, D_FF:].astype(o_ref.dtype)
    o_ref[:, :, D_FF:] = jnp.zeros((2, o_ref.shape[1], D_FF_PAD - D_FF), o_ref.dtype)


def _ffn_up_to_bf16(w, layer):
    _, k, n = w.shape
    tr = _cast_rows(k, n)
    return pl.pallas_call(
        _ffn_up_cast_kernel,
        grid=(k // tr,),
        in_specs=[pl.BlockSpec((None, tr, n), lambda r: (layer, r, 0))],
        out_specs=pl.BlockSpec((2, tr, D_FF_PAD), lambda r: (0, r, 0)),
        out_shape=jax.ShapeDtypeStruct((2, k, D_FF_PAD), BF16),
        compiler_params=_params("parallel"),
    )(w)


def _ffn_down_cast_kernel(x_ref, o_ref, *, tr):
    rows = lax.broadcasted_iota(jnp.int32, x_ref.shape, 0) + pl.program_id(0) * tr
    o_ref[...] = jnp.where(rows < D_FF, x_ref[...], 0.0).astype(o_ref.dtype)


def _ffn_down_to_bf16(w, layer):
    _, _, n = w.shape
    tr = FFN_COL_TILE
    return pl.pallas_call(
        functools.partial(_ffn_down_cast_kernel, tr=tr),
        grid=(D_FF_PAD // tr,),
        in_specs=[pl.BlockSpec((None, tr, n), lambda r: (layer, r, 0))],
        out_specs=pl.BlockSpec((tr, n), lambda r: (r, 0)),
        out_shape=jax.ShapeDtypeStruct((D_FF_PAD, n), BF16),
        compiler_params=_params("parallel"),
    )(w)


def _pad_cols(w, n):
    return jnp.pad(w, ((0, 0), (0, n - w.shape[1])))


def _pad_rows(w, n):
    return jnp.pad(w, ((0, n - w.shape[0]), (0, 0)))


def kernel(x, p, ln_gain, ln_bias, ret_w_in, ret_gn_gain, ret_w_out, swa_w_qkv, swa_sinks, swa_w_out, rel_bias, rwkv_mix, rwkv_w_rkv, rwkv_w0, rwkv_w1, rwkv_w2, rwkv_a0, rwkv_a1, rwkv_a2, rwkv_g1, rwkv_g2, rwkv_k_k, rwkv_k_a, rwkv_r_k, rwkv_gn_gain, rwkv_gn_bias, rwkv_w_out, ffn_w_up, ffn_conv_w, ffn_conv_b, ffn_w_down, ple_w_proj, ple_w_gate):
    batch, seq, d = x.shape
    m = batch * seq
    xf = x.reshape(m, d)
    xb = _to_bf16(xf.reshape(1, m, d), 0)
    row = lambda v: v.reshape(1, -1)

    half = RET_QK_DIM // 2
    inv_freq = 1.0 / (ROPE_BASE ** (jnp.arange(0, RET_QK_DIM, 2, dtype=F32) / RET_QK_DIM))
    ang = jnp.arange(seq, dtype=F32)[:, None] * inv_freq[None, :]
    cos, sin = jnp.cos(ang), jnp.sin(ang)
    log_gamma = jnp.log(1.0 - 2.0 ** (-5.0 - jnp.arange(RET_HEADS, dtype=F32)))
    lg = jnp.broadcast_to(log_gamma[:, None, None], (RET_HEADS, 1, RET_V_DIM))

    for i in range(DEPTH):
        kind, j = i % N_MIXERS, i // N_MIXERS
        g1, b1 = row(ln_gain[i, 0]), row(ln_bias[i, 0])
        if kind == 0:
            proj = _ret_proj(xb, _to_bf16(ret_w_in, j), cos, sin, seq=seq)
            o = _ret_core(proj, lg, row(ret_gn_gain[j]), batch=batch, seq=seq, ts=(1024, 2048)[j])
            xf, xb = _matmul_ln(o, _to_bf16(ret_w_out, j), xf, g1, b1, tm=256)
        elif kind == 1:
            qkv = _matmul(xb, _to_bf16(swa_w_qkv, j), tm=1024, tn=1280, out_dtype=BF16)
            o = _swa_core(qkv, _bias_table(rel_bias), swa_sinks[j], seq=seq)
            xf, xb = _matmul_ln(o, _to_bf16(swa_w_out, j), xf, g1, b1, tm=512)
        else:
            mix = rwkv_mix[j]
            rkv = _rwkv_rkv(xf, mix[jnp.array([0, 2, 3])].reshape(3, 1, d),
                            _all_to_bf16(rwkv_w_rkv[j]), seq=seq)
            lora = RWKV_HEAD_DIM * 2
            ld, a, g = _rwkv_lora(
                xf, mix[jnp.array([1, 4, 5])], row(rwkv_w0[j]), row(rwkv_a0[j]),
                _pad_cols(rwkv_w1[j], lora).astype(BF16), _pad_rows(rwkv_w2[j], lora).astype(BF16),
                _pad_cols(rwkv_a1[j], lora).astype(BF16), _pad_rows(rwkv_a2[j], lora).astype(BF16),
                rwkv_g1[j].astype(BF16), rwkv_g2[j].astype(BF16), seq=seq)
            par = jnp.stack([rwkv_k_k[j], rwkv_k_a[j], rwkv_r_k[j].reshape(-1), rwkv_gn_gain[j],
                             rwkv_gn_bias[j]] + [jnp.zeros((d,), F32)] * 3)
            o = _rwkv_core(rkv, ld, a, g, par, batch=batch, seq=seq)
            xf, xb = _matmul_ln(o, _to_bf16(rwkv_w_out, j), xf, g1, b1, tm=256)

        conv = jnp.concatenate([ffn_conv_w[i], ffn_conv_b[i][None]], axis=0)
        conv = jnp.pad(conv, ((0, SUBLANES - conv.shape[0]), (0, 0)))
        conv = jnp.stack([_pad_cols(conv[:, :D_FF], D_FF_PAD), _pad_cols(conv[:, D_FF:], D_FF_PAD)])
        ffn_variant = (dict(), dict(sub=512), dict(hoist_x=True), dict(row_split=2))[i]
        xf, xb = _ffn(xb, xf, _ffn_up_to_bf16(ffn_w_up, i), conv, _ffn_down_to_bf16(ffn_w_down, i),
                      row(ln_gain[i, 1]), row(ln_bias[i, 1]), seq=seq, **ffn_variant)
        xf, xb = _ple(xb, xf, p[i].reshape(m, PLE_DIM), _to_bf16(ple_w_gate, i),
                      _to_bf16(ple_w_proj, i), tm=(512, 256, 512, 512)[i])
    return xf.reshape(batch, seq, d)
```

```python
import functools
import math

import numpy as np
import jax
import jax.numpy as jnp
from jax import lax
from jax.experimental import pallas as pl
from jax.experimental.pallas import tpu as pltpu

F32 = jnp.float32
BF16 = jnp.bfloat16

D_MODEL = 2048
DEPTH = 4
N_MIXERS = 3

RET_HEADS = 8
RET_QK_DIM = D_MODEL // RET_HEADS
RET_V_DIM = 2 * D_MODEL // RET_HEADS
RET_CHUNK = 128
RET_GN_EPS = 1e-5
ROPE_BASE = 10000.0

SWA_HEAD_DIM = 64
SWA_Q_HEADS = D_MODEL // SWA_HEAD_DIM
SWA_KV_HEADS = SWA_Q_HEADS // 8
SWA_GROUP = SWA_Q_HEADS // SWA_KV_HEADS
SWA_WINDOW = 128
REL_BUCKETS = 32
REL_MAX_DIST = SWA_WINDOW
NEG_INF = -1e30

RWKV_HEAD_DIM = 64
RWKV_HEADS = D_MODEL // RWKV_HEAD_DIM
RWKV_GN_EPS = 64e-5
RWKV_CHUNK = 64

D_FF = 5504
CONV_WIDTH = 3
PLE_DIM = 256

LN_EPS = 1e-5
DEEPNORM_ALPHA = (2.0 * DEPTH) ** 0.25

LANES = 128
SUBLANES = 8
BF16_ROWS = 16
VMEM_LIMIT_BYTES = 56 * 1024 * 1024
FFN_COL_TILE = 512
D_FF_PAD = -(-D_FF // FFN_COL_TILE) * FFN_COL_TILE


def _params(*semantics, flags=None):
    return pltpu.CompilerParams(dimension_semantics=semantics,
                                vmem_limit_bytes=VMEM_LIMIT_BYTES, flags=flags)


def _dot(a, b):
    return jnp.dot(a, b, preferred_element_type=F32)


def _dot_nt(a, b):
    return lax.dot_general(a, b, (((1,), (1,)), ((), ())), preferred_element_type=F32)


def _dot_tn(a, b):
    return lax.dot_general(a, b, (((0,), (0,)), ((), ())), preferred_element_type=F32)


def _layer_norm_rows(z, gain, bias):
    mu = jnp.mean(z, axis=-1, keepdims=True)
    zc = z - mu
    var = jnp.mean(zc * zc, axis=-1, keepdims=True)
    return zc * lax.rsqrt(var + LN_EPS) * gain + bias


def _sigmoid(x):
    return 1.0 / (1.0 + jnp.exp(-x))


def _silu(x):
    return x * _sigmoid(x)


def _matmul_kernel(a_ref, w_ref, o_ref):
    o_ref[...] = _dot(a_ref[...], w_ref[...]).astype(o_ref.dtype)


def _matmul(a, w, *, tm, tn, out_dtype):
    m, k = a.shape
    n = w.shape[1]
    return pl.pallas_call(
        _matmul_kernel,
        grid=(m // tm, n // tn),
        in_specs=[pl.BlockSpec((tm, k), lambda i, j: (i, 0)),
                  pl.BlockSpec((k, tn), lambda i, j: (0, j))],
        out_specs=pl.BlockSpec((tm, tn), lambda i, j: (i, j)),
        out_shape=jax.ShapeDtypeStruct((m, n), out_dtype),
        compiler_params=_params("parallel", "arbitrary"),
    )(a, w)


def _matmul_ln_kernel(a_ref, w_ref, res_ref, gain_ref, bias_ref, xo_ref, xb_ref):
    y = _dot(a_ref[...], w_ref[...])
    xn = _layer_norm_rows(DEEPNORM_ALPHA * res_ref[...] + y, gain_ref[...], bias_ref[...])
    xo_ref[...] = xn
    xb_ref[...] = xn.astype(BF16)


def _matmul_ln(a, w, res, gain, bias, *, tm):
    m, k = a.shape
    d = w.shape[1]
    row = lambda i: (i, 0)
    const = lambda i: (0, 0)
    return pl.pallas_call(
        _matmul_ln_kernel,
        grid=(m // tm,),
        in_specs=[pl.BlockSpec((tm, k), row),
                  pl.BlockSpec((k, d), const),
                  pl.BlockSpec((tm, d), row),
                  pl.BlockSpec((1, d), const),
                  pl.BlockSpec((1, d), const)],
        out_specs=[pl.BlockSpec((tm, d), row), pl.BlockSpec((tm, d), row)],
        out_shape=[jax.ShapeDtypeStruct((m, d), F32), jax.ShapeDtypeStruct((m, d), BF16)],
        compiler_params=_params("parallel"),
    )(a, w, res, gain, bias)


def _ffn_kernel(xb_ref, halo_ref, res_ref, wu_ref, wg_ref, c_ref, wd_ref,
                gain_ref, bias_ref, xo_ref, xob_ref, acc_ref, x_ref,
                *, tiles_per_seq, nf, tf, sub, resident_conv):
    i = pl.program_id(0)
    f = pl.program_id(1)

    @pl.when(f == 0)
    def _():
        halo = jnp.where(i % tiles_per_seq == 0, jnp.zeros_like(halo_ref[...]), halo_ref[...])
        x_ref[...] = jnp.concatenate([halo, xb_ref[...]], axis=0)
        acc_ref[...] = jnp.zeros_like(acc_ref)

    def conv_branch(w_ref, which, cols):
        h = _dot(x_ref[...], w_ref[:, cols])
        c = c_ref[f, which, :, cols] if resident_conv else c_ref[which, :, cols]
        out = c[3:4] + h[BF16_ROWS:] * c[2:3]
        out = out + pltpu.roll(h, 1, axis=0)[BF16_ROWS:] * c[1:2]
        out = out + pltpu.roll(h, 2, axis=0)[BF16_ROWS:] * c[0:1]
        return out

    acts = []
    for s in range(tf // sub):
        cols = slice(s * sub, (s + 1) * sub)
        u = conv_branch(wu_ref, 0, cols)
        gate = conv_branch(wg_ref, 1, cols)
        acts.append((_silu(gate) * u).astype(BF16))
    acc_ref[...] += _dot(jnp.concatenate(acts, axis=1), wd_ref[...])

    @pl.when(f == nf - 1)
    def _():
        xn = _layer_norm_rows(DEEPNORM_ALPHA * res_ref[...] + acc_ref[...], gain_ref[...],
                              bias_ref[...])
        xo_ref[...] = xn
        xob_ref[...] = xn.astype(BF16)


def _ffn(xb, xres, w_ug, c_ug, wd, gain, bias, *, seq, tm=512, tf=FFN_COL_TILE, sub=256,
         resident_conv=False):
    m, d = xb.shape
    fp = w_ug.shape[2]
    nf = fp // tf
    halo_blocks = tm // BF16_ROWS
    row = lambda i, f: (i, 0)
    const = lambda i, f: (0, 0)
    half = lambda which: pl.BlockSpec((None, d, tf), lambda i, f: (which, 0, f))
    if resident_conv:
        c_ug = c_ug.reshape(2, SUBLANES, nf, tf).transpose(2, 0, 1, 3)
        c_spec = pl.BlockSpec((nf, 2, SUBLANES, tf), lambda i, f: (0, 0, 0, 0))
    else:
        c_spec = pl.BlockSpec((2, SUBLANES, tf), lambda i, f: (0, 0, f))
    return pl.pallas_call(
        functools.partial(_ffn_kernel, tiles_per_seq=seq // tm, nf=nf, tf=tf, sub=sub,
                          resident_conv=resident_conv),
        grid=(m // tm, nf),
        in_specs=[pl.BlockSpec((tm, d), row),
                  pl.BlockSpec((BF16_ROWS, d), lambda i, f: (jnp.maximum(i * halo_blocks - 1, 0), 0)),
                  pl.BlockSpec((tm, d), row),
                  half(0), half(1), c_spec,
                  pl.BlockSpec((tf, d), lambda i, f: (f, 0)),
                  pl.BlockSpec((1, d), const),
                  pl.BlockSpec((1, d), const)],
        out_specs=[pl.BlockSpec((tm, d), row), pl.BlockSpec((tm, d), row)],
        out_shape=[jax.ShapeDtypeStruct((m, d), F32), jax.ShapeDtypeStruct((m, d), BF16)],
        scratch_shapes=[pltpu.VMEM((tm, d), F32), pltpu.VMEM((BF16_ROWS + tm, d), BF16)],
        compiler_params=_params("parallel", "arbitrary"),
    )(xb, xb, xres, w_ug, w_ug, c_ug, wd, gain, bias)


def _ple_kernel(xb_ref, wg_ref, p_ref, wp_ref, res_ref, xo_ref, xob_ref):
    gate = _sigmoid(_dot(xb_ref[...], wg_ref[...]))
    proj = _dot(p_ref[...].astype(BF16), wp_ref[...])
    xn = res_ref[...] + proj * gate
    xo_ref[...] = xn
    xob_ref[...] = xn.astype(BF16)


def _ple(xb, xres, p, wg, wp, *, tm=512, tn=2048):
    m, d = xb.shape
    pd = p.shape[1]
    return pl.pallas_call(
        _ple_kernel,
        grid=(m // tm, d // tn),
        in_specs=[pl.BlockSpec((tm, d), lambda i, j: (i, 0)),
                  pl.BlockSpec((d, tn), lambda i, j: (0, j)),
                  pl.BlockSpec((tm, pd), lambda i, j: (i, 0)),
                  pl.BlockSpec((pd, tn), lambda i, j: (0, j)),
                  pl.BlockSpec((tm, tn), lambda i, j: (i, j))],
        out_specs=[pl.BlockSpec((tm, tn), lambda i, j: (i, j)),
                   pl.BlockSpec((tm, tn), lambda i, j: (i, j))],
        out_shape=[jax.ShapeDtypeStruct((m, d), F32), jax.ShapeDtypeStruct((m, d), BF16)],
        compiler_params=_params("parallel", "arbitrary"),
    )(xb, wg, p, wp, xres)


def _ret_proj_kernel(a_ref, w_ref, cos_ref, sin_ref, o_ref, *, tn, nq, nkv):
    j = pl.program_id(1)
    acc = _dot(a_ref[...], w_ref[...])
    half = RET_QK_DIM // 2

    def rotary(scale):
        cos = cos_ref[...]
        sin = sin_ref[...]
        for h in range(tn // RET_QK_DIM):
            lo = h * RET_QK_DIM
            x1 = acc[:, lo:lo + half]
            x2 = acc[:, lo + half:lo + RET_QK_DIM]
            o_ref[:, lo:lo + half] = ((x1 * cos - x2 * sin) * scale).astype(o_ref.dtype)
            o_ref[:, lo + half:lo + RET_QK_DIM] = ((x1 * sin + x2 * cos) * scale).astype(o_ref.dtype)

    @pl.when(j < nq)
    def _():
        rotary(1.0)

    @pl.when((j >= nq) & (j < 2 * nq))
    def _():
        rotary(RET_QK_DIM ** -0.5)

    @pl.when((j >= 2 * nq) & (j < 2 * nq + nkv))
    def _():
        o_ref[...] = acc.astype(o_ref.dtype)

    @pl.when(j >= 2 * nq + nkv)
    def _():
        o_ref[...] = _silu(acc).astype(o_ref.dtype)


def _ret_proj(xb, w, cos, sin, *, seq, tm=1024, tn=2048):
    m, d = xb.shape
    n = w.shape[1]
    nq = RET_HEADS * RET_QK_DIM // tn
    nkv = RET_HEADS * RET_V_DIM // tn
    tiles_per_seq = seq // tm
    return pl.pallas_call(
        functools.partial(_ret_proj_kernel, tn=tn, nq=nq, nkv=nkv),
        grid=(m // tm, n // tn),
        in_specs=[pl.BlockSpec((tm, d), lambda i, j: (i, 0)),
                  pl.BlockSpec((d, tn), lambda i, j: (0, j)),
                  pl.BlockSpec((tm, RET_QK_DIM // 2), lambda i, j: (i % tiles_per_seq, 0)),
                  pl.BlockSpec((tm, RET_QK_DIM // 2), lambda i, j: (i % tiles_per_seq, 0))],
        out_specs=pl.BlockSpec((tm, tn), lambda i, j: (i, j)),
        out_shape=jax.ShapeDtypeStruct((m, n), BF16),
        compiler_params=_params("parallel", "arbitrary"),
    )(xb, w, cos, sin)


def _ret_core_kernel(q_ref, k_ref, v_ref, g_ref, lg_ref, gain_ref, o_ref, state_ref, *, ts):
    c = RET_CHUNK

    @pl.when(pl.program_id(2) == 0)
    def _():
        state_ref[...] = jnp.zeros_like(state_ref)

    lg = lg_ref[0]
    ii = lax.broadcasted_iota(jnp.int32, (c, c), 0)
    jj = lax.broadcasted_iota(jnp.int32, (c, c), 1)
    diff = (ii - jj).astype(F32)
    decay_mask = jnp.where(ii >= jj, jnp.exp(lg[:, :c] * jnp.maximum(diff, 0.0)), 0.0)
    row_v = lax.broadcasted_iota(jnp.int32, (c, RET_V_DIM), 0).astype(F32)
    row_k = lax.broadcasted_iota(jnp.int32, (c, RET_QK_DIM), 0).astype(F32)
    q_decay = jnp.exp(lg * (row_v + 1.0))
    k_decay = jnp.exp(lg[:, :RET_QK_DIM] * (c - 1.0 - row_k))
    chunk_decay = jnp.exp(lg * float(c))
    gain = gain_ref[...]

    chunks = range(ts // c)
    rows = [slice(n * c, (n + 1) * c) for n in chunks]
    scores = [(_dot_nt(q_ref[rows[n], :], k_ref[rows[n], :]) * decay_mask).astype(BF16)
              for n in chunks]
    inner = [_dot(scores[n], v_ref[rows[n], :]) for n in chunks]
    kv = [_dot_tn((k_ref[rows[n], :].astype(F32) * k_decay).astype(BF16), v_ref[rows[n], :])
          for n in chunks]
    state = state_ref[...]
    states = []
    for n in chunks:
        states.append(state.astype(BF16))
        state = state * chunk_decay + kv[n]
    state_ref[...] = state
    cross = [_dot(q_ref[rows[n], :], states[n]) * q_decay for n in chunks]
    for n in chunks:
        o = inner[n] + cross[n]
        mu = jnp.mean(o, axis=-1, keepdims=True)
        oc = o - mu
        var = jnp.mean(oc * oc, axis=-1, keepdims=True)
        on = oc * lax.rsqrt(var + RET_GN_EPS) * gain
        o_ref[rows[n], :] = (g_ref[rows[n], :].astype(F32) * on).astype(o_ref.dtype)


def _ret_core(proj, lg, gn_gain, *, batch, seq, ts=1024):
    m = proj.shape[0]
    nt = seq // ts
    kq = RET_HEADS
    row = lambda b, h, t: b * nt + t
    return pl.pallas_call(
        functools.partial(_ret_core_kernel, ts=ts),
        grid=(batch, RET_HEADS, nt),
        in_specs=[pl.BlockSpec((ts, RET_QK_DIM), lambda b, h, t: (row(b, h, t), h)),
                  pl.BlockSpec((ts, RET_QK_DIM), lambda b, h, t: (row(b, h, t), kq + h)),
                  pl.BlockSpec((ts, RET_V_DIM), lambda b, h, t: (row(b, h, t), kq + h)),
                  pl.BlockSpec((ts, RET_V_DIM), lambda b, h, t: (row(b, h, t), 2 * kq + h)),
                  pl.BlockSpec((1, 1, RET_V_DIM), lambda b, h, t: (h, 0, 0)),
                  pl.BlockSpec((1, RET_V_DIM), lambda b, h, t: (0, h))],
        out_specs=pl.BlockSpec((ts, RET_V_DIM), lambda b, h, t: (row(b, h, t), h)),
        out_shape=jax.ShapeDtypeStruct((m, RET_HEADS * RET_V_DIM), BF16),
        scratch_shapes=[pltpu.VMEM((RET_QK_DIM, RET_V_DIM), F32)],
        compiler_params=_params("parallel", "parallel", "arbitrary"),
    )(proj, proj, proj, proj, lg, gn_gain)


def _t5_buckets(dist):
    n = np.maximum(dist, 0)
    max_exact = REL_BUCKETS // 2
    large = max_exact + (np.log(np.maximum(n, 1) / max_exact) / np.log(REL_MAX_DIST / max_exact)
                         * (REL_BUCKETS - max_exact)).astype(np.int32)
    large = np.minimum(large, REL_BUCKETS - 1)
    return np.where(n < max_exact, n, large).astype(np.int32)


def _bias_table_kernel(rel_ref, bucket_ref, o_ref):
    h = pl.program_id(0)
    buckets = bucket_ref[...]
    table = jnp.zeros(buckets.shape, F32)
    for b in range(REL_BUCKETS):
        table = jnp.where(buckets == b, rel_ref[b, h], table)
    o_ref[0] = table


def _bias_table(rel_bias):
    c = SWA_WINDOW
    dist = np.arange(c)[:, None] + c - np.arange(2 * c)[None, :]
    buckets = jnp.asarray(_t5_buckets(dist))
    return pl.pallas_call(
        _bias_table_kernel,
        grid=(SWA_Q_HEADS,),
        in_specs=[pl.BlockSpec(memory_space=pltpu.SMEM),
                  pl.BlockSpec((c, 2 * c), lambda h: (0, 0))],
        out_specs=pl.BlockSpec((1, c, 2 * c), lambda h: (h, 0, 0)),
        out_shape=jax.ShapeDtypeStruct((SWA_Q_HEADS, c, 2 * c), F32),
        compiler_params=_params("arbitrary"),
    )(rel_bias, buckets)


def _swa_kernel(q_ref, kp_ref, kc_ref, vp_ref, vc_ref, bias_ref, sink_ref, o_ref, *, blocks_per_seq):
    c = SWA_WINDOW
    hd = SWA_HEAD_DIM
    has_prev = (pl.program_id(0) % blocks_per_seq) > 0
    ii = lax.broadcasted_iota(jnp.int32, (c, 2 * c), 0)
    jj = lax.broadcasted_iota(jnp.int32, (c, 2 * c), 1)
    ok = ((jj < c) & (jj > ii) & has_prev) | ((jj >= c) & (jj - c <= ii))
    left = lax.broadcasted_iota(jnp.int32, (c, LANES), 1) < hd
    ones = jnp.ones((2 * c, LANES), BF16)
    zeros = jnp.zeros((2 * c, hd), BF16)
    scale = hd ** -0.5
    for g in range(SWA_KV_HEADS):
        cols = slice(g * hd, (g + 1) * hd)
        k = jnp.concatenate([kp_ref[:, cols], kc_ref[:, cols]], axis=0)
        v = jnp.concatenate([vp_ref[:, cols], vc_ref[:, cols]], axis=0)
        k_half = (jnp.concatenate([k, zeros], axis=1), jnp.concatenate([zeros, k], axis=1))
        v_half = (jnp.concatenate([v, zeros], axis=1), jnp.concatenate([zeros, v], axis=1))
        heads = range(g * SWA_GROUP, (g + 1) * SWA_GROUP)
        s = [_dot_nt(q_ref[:, (h // 2) * LANES:(h // 2 + 1) * LANES], k_half[h % 2]) for h in heads]
        s = [jnp.where(ok, s[i] * scale + bias_ref[h], NEG_INF) for i, h in enumerate(heads)]
        mx = [jnp.maximum(jnp.max(s[i], axis=-1, keepdims=True), sink_ref[h])
              for i, h in enumerate(heads)]
        p = [jnp.exp(s[i] - mx[i]).astype(BF16) for i in range(SWA_GROUP)]
        den = [_dot(p[i], ones) + jnp.exp(sink_ref[h] - mx[i]) for i, h in enumerate(heads)]
        for i in range(0, SWA_GROUP, 2):
            num = _dot(p[i], v_half[0]) + _dot(p[i + 1], v_half[1])
            out = num / jnp.where(left, den[i], den[i + 1])
            lo = (g * SWA_GROUP + i) * hd
            o_ref[:, lo:lo + LANES] = out.astype(o_ref.dtype)


def _swa_core(qkv, bias, sinks, *, seq):
    m = qkv.shape[0]
    c = SWA_WINDOW
    blocks_per_seq = seq // c
    qw = SWA_Q_HEADS * SWA_HEAD_DIM
    kw = SWA_KV_HEADS * SWA_HEAD_DIM
    kcol = qw // kw
    prev = lambda r: jnp.where(r % blocks_per_seq == 0, r, r - 1)
    return pl.pallas_call(
        functools.partial(_swa_kernel, blocks_per_seq=blocks_per_seq),
        grid=(m // c,),
        in_specs=[pl.BlockSpec((c, qw), lambda r: (r, 0)),
                  pl.BlockSpec((c, kw), lambda r: (prev(r), kcol)),
                  pl.BlockSpec((c, kw), lambda r: (r, kcol)),
                  pl.BlockSpec((c, kw), lambda r: (prev(r), kcol + 1)),
                  pl.BlockSpec((c, kw), lambda r: (r, kcol + 1)),
                  pl.BlockSpec((SWA_Q_HEADS, c, 2 * c), lambda r: (0, 0, 0)),
                  pl.BlockSpec(memory_space=pltpu.SMEM)],
        out_specs=pl.BlockSpec((c, qw), lambda r: (r, 0)),
        out_shape=jax.ShapeDtypeStruct((m, qw), BF16),
        compiler_params=_params("parallel"),
    )(qkv, qkv, qkv, qkv, qkv, bias, sinks)


def _shifted_delta(x, halo_ref, at_seq_start):
    first = jnp.where(at_seq_start, jnp.zeros_like(halo_ref[SUBLANES - 1:, :]), halo_ref[SUBLANES - 1:, :])
    rows = lax.broadcasted_iota(jnp.int32, x.shape, 0)
    prev = jnp.where(rows == 0, first, pltpu.roll(x, 1, axis=0))
    return prev - x


def _rwkv_rkv_kernel(x_ref, halo_ref, mix_ref, w_ref, o_ref, amix_ref, *, tiles_per_seq, nj):
    i = pl.program_id(0)
    j = pl.program_id(1)

    @pl.when(j % nj == 0)
    def _():
        x = x_ref[...]
        xx = _shifted_delta(x, halo_ref, i % tiles_per_seq == 0)
        amix_ref[...] = (x + xx * mix_ref[0]).astype(BF16)

    o_ref[...] = _dot(amix_ref[...], w_ref[...])


def _rwkv_rkv(x, mix3, w3, *, seq, tm=512, tn=2048):
    m, d = x.shape
    nj = d // tn
    halo_blocks = tm // SUBLANES
    return pl.pallas_call(
        functools.partial(_rwkv_rkv_kernel, tiles_per_seq=seq // tm, nj=nj),
        grid=(m // tm, 3 * nj),
        in_specs=[pl.BlockSpec((tm, d), lambda i, j: (i, 0)),
                  pl.BlockSpec((SUBLANES, d), lambda i, j: (jnp.maximum(i * halo_blocks - 1, 0), 0)),
                  pl.BlockSpec((1, 1, d), lambda i, j: (j // nj, 0, 0)),
                  pl.BlockSpec((None, d, tn), lambda i, j: (j // nj, 0, j % nj))],
        out_specs=pl.BlockSpec((None, tm, tn), lambda i, j: (j // nj, i, j % nj)),
        out_shape=jax.ShapeDtypeStruct((3, m, d), F32),
        scratch_shapes=[pltpu.VMEM((tm, d), BF16)],
        compiler_params=_params("parallel", "arbitrary"),
    )(x, x, mix3, w3)


def _rwkv_lora_kernel(x_ref, halo_ref, mix_ref, w0_ref, a0_ref, w1_ref, w2_ref, a1_ref, a2_ref,
                      g1_ref, g2_ref, ld_ref, a_ref, g_ref, *, tiles_per_seq):
    i = pl.program_id(0)
    x = x_ref[...]
    xx = _shifted_delta(x, halo_ref, i % tiles_per_seq == 0)
    mix = mix_ref[...]
    xw = (x + xx * mix[0:1]).astype(BF16)
    xa = (x + xx * mix[1:2]).astype(BF16)
    xg = (x + xx * mix[2:3]).astype(BF16)
    z = w0_ref[...] + _dot(jnp.tanh(_dot(xw, w1_ref[...])).astype(BF16), w2_ref[...])
    ld_ref[...] = -math.exp(-0.5) * _sigmoid(z)
    a_ref[...] = _sigmoid(a0_ref[...] + _dot(_dot(xa, a1_ref[...]).astype(BF16), a2_ref[...]))
    g_ref[...] = _dot(_sigmoid(_dot(xg, g1_ref[...])).astype(BF16), g2_ref[...]).astype(g_ref.dtype)


def _rwkv_lora(x, mix3, w0, a0, w1, w2, a1, a2, g1, g2, *, seq, tm=512):
    m, d = x.shape
    halo_blocks = tm // SUBLANES
    row = lambda i: (i, 0)
    full = lambda a: pl.BlockSpec(a.shape, lambda i: (0, 0))
    return pl.pallas_call(
        functools.partial(_rwkv_lora_kernel, tiles_per_seq=seq // tm),
        grid=(m // tm,),
        in_specs=[pl.BlockSpec((tm, d), row),
                  pl.BlockSpec((SUBLANES, d), lambda i: (jnp.maximum(i * halo_blocks - 1, 0), 0)),
                  full(mix3), full(w0), full(a0), full(w1), full(w2), full(a1), full(a2),
                  full(g1), full(g2)],
        out_specs=[pl.BlockSpec((tm, d), row)] * 3,
        out_shape=[jax.ShapeDtypeStruct((m, d), F32), jax.ShapeDtypeStruct((m, d), F32),
                   jax.ShapeDtypeStruct((m, d), BF16)],
        compiler_params=_params("parallel"),
    )(x, x, mix3, w0, a0, w1, w2, a1, a2, g1, g2)


def _pair_mask(n):
    rows = lax.broadcasted_iota(jnp.int32, (2 * n, 2 * n), 0)
    cols = lax.broadcasted_iota(jnp.int32, (2 * n, 2 * n), 1)
    return (rows < n) == (cols < n)


def _block_diag(y, same_head):
    return jnp.where(same_head, jnp.concatenate([y, y], axis=0), 0.0).astype(BF16)


def _fold_diag(x, n, left):
    return jnp.where(left, x[:n], x[n:])


def _segment_sum(x, ones_bd):
    hi = x.astype(BF16)
    lo = (x - hi.astype(F32)).astype(BF16)
    return _dot(hi, ones_bd) + _dot(lo, ones_bd)


def _chunk_cumsum(x, n):
    pos = lax.broadcasted_iota(jnp.int32, x.shape, 0) % n
    shift = 1
    while shift < n:
        x = x + jnp.where(pos >= shift, pltpu.roll(x, shift, axis=0), 0.0)
        shift *= 2
    return x


def _rwkv_core_kernel(r_ref, k_ref, v_ref, ld_ref, a_ref, g_ref, par_ref, o_ref, state_ref, *, tc):
    hd = RWKV_HEAD_DIM
    n = RWKV_CHUNK
    assert n == hd and 2 * hd == LANES
    bf = lambda t: t.astype(BF16)

    @pl.when(pl.program_id(2) == 0)
    def _():
        state_ref[...] = jnp.zeros_like(state_ref)

    same_head = _pair_mask(n)
    ones_bd = jnp.where(same_head, 1.0, 0.0).astype(BF16)
    lane = lax.broadcasted_iota(jnp.int32, (n, LANES), 1)
    row = lax.broadcasted_iota(jnp.int32, (n, LANES), 0)
    left = lane < hd
    pos = lane % hd
    strict = row > pos
    incl = row >= pos
    diag = row == pos

    pairs = range(r_ref.shape[1] // LANES)
    par = par_ref[...]

    def prepare(p):
        lanes = slice(p * LANES, (p + 1) * LANES)
        kk_w, ka_w, rk_w = (par[i:i + 1, lanes] for i in range(3))
        r, k, v, ld, a = (ref[:, lanes] for ref in (r_ref, k_ref, v_ref, ld_ref, a_ref))
        kk = k * kk_w
        kk = kk / jnp.maximum(jnp.sqrt(_segment_sum(kk * kk, ones_bd)), 1e-12)
        km = k * (1.0 + (a - 1.0) * ka_w)
        bv = kk * a
        cw = _chunk_cumsum(ld, n)
        w_inv = jnp.exp(-cw)
        return (v, km, bv, cw, -kk * jnp.exp(cw - ld), r * jnp.exp(cw), bv * w_inv, km * w_inv,
                _segment_sum(r * km * rk_w, ones_bd) * v)

    v, km, bv, cw, a_t, r_t, b_t, k_t, bonus = (
        jnp.concatenate(parts, axis=0) for parts in zip(*[prepare(p) for p in pairs]))

    chunks = range(len(pairs) * (tc // n))
    rows = [slice(c * n, (c + 1) * n) for c in chunks]
    cw_last = [cw[(c + 1) * n - 1:(c + 1) * n] for c in chunks]
    to_end = [jnp.exp(cw_last[c] - cw[rows[c]]) for c in chunks]
    bk_e = [bf(jnp.concatenate([bv[rows[c]] * to_end[c], km[rows[c]] * to_end[c]], axis=0))
            for c in chunks]
    zeros = jnp.zeros((n, LANES), F32)
    zeros_bd = jnp.zeros((2 * n, 2 * n), BF16)
    bd = lambda t: _block_diag(t, same_head)

    x = [_dot_nt(bf(jnp.concatenate([a_t[rows[c]], r_t[rows[c]]], axis=0)),
                 jnp.concatenate([bd(b_t[rows[c]]), bd(k_t[rows[c]])], axis=0)) for c in chunks]
    a_ab = [jnp.where(strict, x[c][:n, :LANES], 0.0) for c in chunks]
    a_ak = [jnp.where(strict, x[c][:n, LANES:], 0.0) for c in chunks]
    a_rbk = [bf(jnp.concatenate([jnp.where(incl, x[c][n:, :LANES], 0.0),
                                 jnp.where(incl, x[c][n:, LANES:], 0.0)], axis=1)) for c in chunks]

    inv = [jnp.where(diag, 1.0, a_ab[c]) for c in chunks]
    power = [_dot(bf(a_ab[c]), bd(a_ab[c])) for c in chunks]
    for _ in range(int(math.log2(n)) - 2):
        both = [_dot(bf(jnp.concatenate([inv[c], power[c]], axis=0)), bd(power[c])) for c in chunks]
        inv = [inv[c] + both[c][:n] for c in chunks]
        power = [both[c][n:] for c in chunks]
    inv = [inv[c] + _dot(bf(inv[c]), bd(power[c])) for c in chunks]

    av = [_dot(bf(a_ak[c]), bd(v[rows[c]])) for c in chunks]
    au = [_dot(bf(inv[c]), jnp.concatenate([bd(a_t[rows[c]]), bd(av[c])], axis=1))
          for c in chunks]
    gc = [_dot_tn(bk_e[c], bf(jnp.concatenate(
              [au[c], jnp.concatenate([zeros, v[rows[c]]], axis=1)], axis=0))) for c in chunks]
    g_mat = [_fold_diag(gc[c][:, :LANES], n, left) + jnp.where(diag, jnp.exp(cw_last[c]), 0.0)
             for c in chunks]
    c0 = [_fold_diag(gc[c][:, LANES:], n, left) for c in chunks]
    ry = [_dot(a_rbk[c], jnp.concatenate(
              [jnp.concatenate([bd(au[c][:, :LANES]), bd(au[c][:, LANES:])], axis=1),
               jnp.concatenate([zeros_bd, bd(v[rows[c]])], axis=1)], axis=0))
          for c in chunks]
    rg = [bf(jnp.concatenate([r_t[rows[c]] + ry[c][:, :LANES], g_mat[c]], axis=0)) for c in chunks]

    nc = tc // n
    st = [state_ref[p] for p in pairs]
    ys = [[] for _ in pairs]
    for c in range(nc):
        outs = [_dot(rg[p * nc + c], bd(st[p])) for p in pairs]
        for p in pairs:
            ys[p].append(outs[p][:n] + ry[p * nc + c][:, LANES:])
            st[p] = outs[p][n:] + c0[p * nc + c]
    for p in pairs:
        state_ref[p] = st[p]

    for p in pairs:
        lanes = slice(p * LANES, (p + 1) * LANES)
        y = jnp.concatenate(ys[p], axis=0)
        mu = _segment_sum(y, ones_bd) * (1.0 / hd)
        yc = y - mu
        var = _segment_sum(yc * yc, ones_bd) * (1.0 / hd)
        yn = yc * lax.rsqrt(var + RWKV_GN_EPS) * par[3:4, lanes] + par[4:5, lanes]
        o_ref[:, lanes] = ((yn + bonus[p * tc:(p + 1) * tc]) * g_ref[:, lanes].astype(F32)
                           ).astype(o_ref.dtype)


def _rwkv_core(rkv, ld, a, g, par, *, batch, seq, batch0=0, tc=512, pairs=4):
    d = ld.shape[1]
    nt = seq // tc
    width = pairs * LANES
    blk = lambda b, hp, t: ((batch0 + b) * nt + t, hp)
    sel = lambda which: pl.BlockSpec((None, tc, width),
                                     lambda b, hp, t: (which, (batch0 + b) * nt + t, hp))
    return pl.pallas_call(
        functools.partial(_rwkv_core_kernel, tc=tc),
        grid=(batch, d // width, nt),
        in_specs=[sel(0), sel(1), sel(2),
                  pl.BlockSpec((tc, width), blk),
                  pl.BlockSpec((tc, width), blk),
                  pl.BlockSpec((tc, width), blk),
                  pl.BlockSpec((SUBLANES, width), lambda b, hp, t: (0, hp))],
        out_specs=pl.BlockSpec((tc, width), lambda b, hp, t: (b * nt + t, hp)),
        out_shape=jax.ShapeDtypeStruct((batch * seq, d), BF16),
        scratch_shapes=[pltpu.VMEM((pairs, RWKV_HEAD_DIM, LANES), F32)],
        compiler_params=_params("parallel", "parallel", "arbitrary"),
    )(rkv, rkv, rkv, ld, a, g, par)


CAST_BLOCK_BYTES = 8 * 1024 * 1024


def _cast_rows(k, n):
    tr = k
    while tr * n * 4 > CAST_BLOCK_BYTES and tr % 2 == 0 and (tr // 2) % BF16_ROWS == 0:
        tr //= 2
    return tr


def _cast_kernel(x_ref, o_ref):
    o_ref[...] = x_ref[...].astype(o_ref.dtype)


def _to_bf16(w, layer):
    _, k, n = w.shape
    tr = _cast_rows(k, n)
    return pl.pallas_call(
        _cast_kernel,
        grid=(k // tr,),
        in_specs=[pl.BlockSpec((None, tr, n), lambda r: (layer, r, 0))],
        out_specs=pl.BlockSpec((tr, n), lambda r: (r, 0)),
        out_shape=jax.ShapeDtypeStruct((k, n), BF16),
        compiler_params=_params("parallel"),
    )(w)


def _all_to_bf16(w):
    layers, k, n = w.shape
    tr = _cast_rows(k, n)
    return pl.pallas_call(
        _cast_kernel,
        grid=(layers, k // tr),
        in_specs=[pl.BlockSpec((None, tr, n), lambda l, r: (l, r, 0))],
        out_specs=pl.BlockSpec((None, tr, n), lambda l, r: (l, r, 0)),
        out_shape=jax.ShapeDtypeStruct((layers, k, n), BF16),
        compiler_params=_params("parallel", "parallel"),
    )(w)


def _ffn_up_cast_kernel(x_ref, o_ref):
    o_ref[0, :, :D_FF] = x_ref[:, :D_FF].astype(o_ref.dtype)
    o_ref[1, :, :D_FF] = x_ref[:, D_FF:].astype(o_ref.dtype)
    o_ref[:, :, D_FF:] = jnp.zeros((2, o_ref.shape[1], D_FF_PAD - D_FF), o_ref.dtype)


def _ffn_up_to_bf16(w, layer):
    _, k, n = w.shape
    tr = _cast_rows(k, n)
    return pl.pallas_call(
        _ffn_up_cast_kernel,
        grid=(k // tr,),
        in_specs=[pl.BlockSpec((None, tr, n), lambda r: (layer, r, 0))],
        out_specs=pl.BlockSpec((2, tr, D_FF_PAD), lambda r: (0, r, 0)),
        out_shape=jax.ShapeDtypeStruct((2, k, D_FF_PAD), BF16),
        compiler_params=_params("parallel"),
    )(w)


def _ffn_down_cast_kernel(x_ref, o_ref, *, tr):
    rows = lax.broadcasted_iota(jnp.int32, x_ref.shape, 0) + pl.program_id(0) * tr
    o_ref[...] = jnp.where(rows < D_FF, x_ref[...], 0.0).astype(o_ref.dtype)


def _ffn_down_to_bf16(w, layer):
    _, _, n = w.shape
    tr = FFN_COL_TILE
    return pl.pallas_call(
        functools.partial(_ffn_down_cast_kernel, tr=tr),
        grid=(D_FF_PAD // tr,),
        in_specs=[pl.BlockSpec((None, tr, n), lambda r: (layer, r, 0))],
        out_specs=pl.BlockSpec((tr, n), lambda r: (r, 0)),
        out_shape=jax.ShapeDtypeStruct((D_FF_PAD, n), BF16),
        compiler_params=_params("parallel"),
    )(w)


def _pad_cols(w, n):
    return jnp.pad(w, ((0, 0), (0, n - w.shape[1])))


def _pad_rows(w, n):
    return jnp.pad(w, ((0, n - w.shape[0]), (0, 0)))


def kernel(x, p, ln_gain, ln_bias, ret_w_in, ret_gn_gain, ret_w_out, swa_w_qkv, swa_sinks, swa_w_out, rel_bias, rwkv_mix, rwkv_w_rkv, rwkv_w0, rwkv_w1, rwkv_w2, rwkv_a0, rwkv_a1, rwkv_a2, rwkv_g1, rwkv_g2, rwkv_k_k, rwkv_k_a, rwkv_r_k, rwkv_gn_gain, rwkv_gn_bias, rwkv_w_out, ffn_w_up, ffn_conv_w, ffn_conv_b, ffn_w_down, ple_w_proj, ple_w_gate):
    batch, seq, d = x.shape
    m = batch * seq
    xf = x.reshape(m, d)
    xb = _to_bf16(xf.reshape(1, m, d), 0)
    row = lambda v: v.reshape(1, -1)

    half = RET_QK_DIM // 2
    inv_freq = 1.0 / (ROPE_BASE ** (jnp.arange(0, RET_QK_DIM, 2, dtype=F32) / RET_QK_DIM))
    ang = jnp.arange(seq, dtype=F32)[:, None] * inv_freq[None, :]
    cos, sin = jnp.cos(ang), jnp.sin(ang)
    log_gamma = jnp.log(1.0 - 2.0 ** (-5.0 - jnp.arange(RET_HEADS, dtype=F32)))
    lg = jnp.broadcast_to(log_gamma[:, None, None], (RET_HEADS, 1, RET_V_DIM))

    for i in range(DEPTH):
        kind, j = i % N_MIXERS, i // N_MIXERS
        g1, b1 = row(ln_gain[i, 0]), row(ln_bias[i, 0])
        if kind == 0:
            proj = _ret_proj(xb, _to_bf16(ret_w_in, j), cos, sin, seq=seq)
            o = _ret_core(proj, lg, row(ret_gn_gain[j]), batch=batch, seq=seq, ts=(2048, 4096)[j])
            xf, xb = _matmul_ln(o, _to_bf16(ret_w_out, j), xf, g1, b1, tm=256)
        elif kind == 1:
            qkv = _matmul(xb, _to_bf16(swa_w_qkv, j), tm=1024, tn=1280, out_dtype=BF16)
            o = _swa_core(qkv, _bias_table(rel_bias), swa_sinks[j], seq=seq)
            xf, xb = _matmul_ln(o, _to_bf16(swa_w_out, j), xf, g1, b1, tm=512)
        else:
            mix = rwkv_mix[j]
            rkv = _rwkv_rkv(xf, mix[jnp.array([0, 2, 3])].reshape(3, 1, d),
                            _all_to_bf16(rwkv_w_rkv[j]), seq=seq)
            lora = RWKV_HEAD_DIM * 2
            ld, a, g = _rwkv_lora(
                xf, mix[jnp.array([1, 4, 5])], row(rwkv_w0[j]), row(rwkv_a0[j]),
                _pad_cols(rwkv_w1[j], lora).astype(BF16), _pad_rows(rwkv_w2[j], lora).astype(BF16),
                _pad_cols(rwkv_a1[j], lora).astype(BF16), _pad_rows(rwkv_a2[j], lora).astype(BF16),
                rwkv_g1[j].astype(BF16), rwkv_g2[j].astype(BF16), seq=seq)
            par = jnp.stack([rwkv_k_k[j], rwkv_k_a[j], rwkv_r_k[j].reshape(-1), rwkv_gn_gain[j],
                             rwkv_gn_bias[j]] + [jnp.zeros((d,), F32)] * 3)
            o = jnp.concatenate([
                _rwkv_core(rkv, ld, a, g, par, batch=1, seq=seq, batch0=b, pairs=(4, 8)[b % 2])
                for b in range(batch)])
            xf, xb = _matmul_ln(o, _to_bf16(rwkv_w_out, j), xf, g1, b1, tm=512)

        conv = jnp.concatenate([ffn_conv_w[i], ffn_conv_b[i][None]], axis=0)
        conv = jnp.pad(conv, ((0, SUBLANES - conv.shape[0]), (0, 0)))
        conv = jnp.stack([_pad_cols(conv[:, :D_FF], D_FF_PAD), _pad_cols(conv[:, D_FF:], D_FF_PAD)])
        ffn_variant = (dict(), dict(sub=512), dict(resident_conv=True),
                       dict(sub=512, resident_conv=True))[i]
        xf, xb = _ffn(xb, xf, _ffn_up_to_bf16(ffn_w_up, i), conv, _ffn_down_to_bf16(ffn_w_down, i),
                      row(ln_gain[i, 1]), row(ln_bias[i, 1]), seq=seq, **ffn_variant)
        xf, xb = _ple(xb, xf, p[i].reshape(m, PLE_DIM), _to_bf16(ple_w_gate, i),
                      _to_bf16(ple_w_proj, i), tm=(512, 256, 512, 512)[i])
    return xf.reshape(batch, seq, d)
```

```python
import functools
import math

import numpy as np
import jax
import jax.numpy as jnp
from jax import lax
from jax.experimental import pallas as pl
from jax.experimental.pallas import tpu as pltpu

F32 = jnp.float32
BF16 = jnp.bfloat16

D_MODEL = 2048
DEPTH = 4
N_MIXERS = 3

RET_HEADS = 8
RET_QK_DIM = D_MODEL // RET_HEADS
RET_V_DIM = 2 * D_MODEL // RET_HEADS
RET_CHUNK = 128
RET_GN_EPS = 1e-5
ROPE_BASE = 10000.0

SWA_HEAD_DIM = 64
SWA_Q_HEADS = D_MODEL // SWA_HEAD_DIM
SWA_KV_HEADS = SWA_Q_HEADS // 8
SWA_GROUP = SWA_Q_HEADS // SWA_KV_HEADS
SWA_WINDOW = 128
REL_BUCKETS = 32
REL_MAX_DIST = SWA_WINDOW
NEG_INF = -1e30

RWKV_HEAD_DIM = 64
RWKV_HEADS = D_MODEL // RWKV_HEAD_DIM
RWKV_GN_EPS = 64e-5
RWKV_CHUNK = 64

D_FF = 5504
CONV_WIDTH = 3
PLE_DIM = 256

LN_EPS = 1e-5
DEEPNORM_ALPHA = (2.0 * DEPTH) ** 0.25

LANES = 128
SUBLANES = 8
BF16_ROWS = 16
VMEM_LIMIT_BYTES = 56 * 1024 * 1024
FFN_COL_TILE = 512
D_FF_PAD = -(-D_FF // FFN_COL_TILE) * FFN_COL_TILE


def _params(*semantics, flags=None):
    return pltpu.CompilerParams(dimension_semantics=semantics,
                                vmem_limit_bytes=VMEM_LIMIT_BYTES, flags=flags)


def _dot(a, b):
    return jnp.dot(a, b, preferred_element_type=F32)


def _dot_nt(a, b):
    return lax.dot_general(a, b, (((1,), (1,)), ((), ())), preferred_element_type=F32)


def _dot_tn(a, b):
    return lax.dot_general(a, b, (((0,), (0,)), ((), ())), preferred_element_type=F32)


def _layer_norm_rows(z, gain, bias):
    mu = jnp.mean(z, axis=-1, keepdims=True)
    zc = z - mu
    var = jnp.mean(zc * zc, axis=-1, keepdims=True)
    return zc * lax.rsqrt(var + LN_EPS) * gain + bias


def _sigmoid(x):
    return 1.0 / (1.0 + jnp.exp(-x))


def _silu(x):
    return x * _sigmoid(x)


def _matmul_kernel(a_ref, w_ref, o_ref):
    o_ref[...] = _dot(a_ref[...], w_ref[...]).astype(o_ref.dtype)


def _matmul(a, w, *, tm, tn, out_dtype):
    m, k = a.shape
    n = w.shape[1]
    return pl.pallas_call(
        _matmul_kernel,
        grid=(m // tm, n // tn),
        in_specs=[pl.BlockSpec((tm, k), lambda i, j: (i, 0)),
                  pl.BlockSpec((k, tn), lambda i, j: (0, j))],
        out_specs=pl.BlockSpec((tm, tn), lambda i, j: (i, j)),
        out_shape=jax.ShapeDtypeStruct((m, n), out_dtype),
        compiler_params=_params("parallel", "arbitrary"),
    )(a, w)


def _matmul_ln_kernel(a_ref, w_ref, res_ref, gain_ref, bias_ref, xo_ref, xb_ref):
    y = _dot(a_ref[...], w_ref[...])
    xn = _layer_norm_rows(DEEPNORM_ALPHA * res_ref[...] + y, gain_ref[...], bias_ref[...])
    xo_ref[...] = xn
    xb_ref[...] = xn.astype(BF16)


def _matmul_ln(a, w, res, gain, bias, *, tm):
    m, k = a.shape
    d = w.shape[1]
    row = lambda i: (i, 0)
    const = lambda i: (0, 0)
    return pl.pallas_call(
        _matmul_ln_kernel,
        grid=(m // tm,),
        in_specs=[pl.BlockSpec((tm, k), row),
                  pl.BlockSpec((k, d), const),
                  pl.BlockSpec((tm, d), row),
                  pl.BlockSpec((1, d), const),
                  pl.BlockSpec((1, d), const)],
        out_specs=[pl.BlockSpec((tm, d), row), pl.BlockSpec((tm, d), row)],
        out_shape=[jax.ShapeDtypeStruct((m, d), F32), jax.ShapeDtypeStruct((m, d), BF16)],
        compiler_params=_params("parallel"),
    )(a, w, res, gain, bias)


def _ffn_kernel(xb_ref, halo_ref, res_ref, wu_ref, wg_ref, c_ref, wd_ref,
                gain_ref, bias_ref, xo_ref, xob_ref, acc_ref, x_ref,
                *, tiles_per_seq, nf):
    i = pl.program_id(0)
    f = pl.program_id(1)

    @pl.when(f == 0)
    def _():
        halo = jnp.where(i % tiles_per_seq == 0, jnp.zeros_like(halo_ref[...]), halo_ref[...])
        x_ref[...] = jnp.concatenate([halo, xb_ref[...]], axis=0)
        acc_ref[...] = jnp.zeros_like(acc_ref)

    def conv_branch(w_ref, which):
        h = _dot(x_ref[...], w_ref[...])
        c = c_ref[which]
        out = c[3:4] + h[BF16_ROWS:] * c[2:3]
        out = out + pltpu.roll(h, 1, axis=0)[BF16_ROWS:] * c[1:2]
        out = out + pltpu.roll(h, 2, axis=0)[BF16_ROWS:] * c[0:1]
        return out

    u = conv_branch(wu_ref, 0)
    gate = conv_branch(wg_ref, 1)
    acc_ref[...] += _dot((_silu(gate) * u).astype(BF16), wd_ref[...])

    @pl.when(f == nf - 1)
    def _():
        xn = _layer_norm_rows(DEEPNORM_ALPHA * res_ref[...] + acc_ref[...], gain_ref[...],
                              bias_ref[...])
        xo_ref[...] = xn
        xob_ref[...] = xn.astype(BF16)


def _ffn(xb, xres, w_ug, c_ug, wd, gain, bias, *, seq, tm=512, tf=FFN_COL_TILE):
    m, d = xb.shape
    nf = w_ug.shape[2] // tf
    halo_blocks = tm // BF16_ROWS
    row = lambda i, f: (i, 0)
    const = lambda i, f: (0, 0)
    half = lambda which: pl.BlockSpec((None, d, tf), lambda i, f: (which, 0, f))
    return pl.pallas_call(
        functools.partial(_ffn_kernel, tiles_per_seq=seq // tm, nf=nf),
        grid=(m // tm, nf),
        in_specs=[pl.BlockSpec((tm, d), row),
                  pl.BlockSpec((BF16_ROWS, d), lambda i, f: (jnp.maximum(i * halo_blocks - 1, 0), 0)),
                  pl.BlockSpec((tm, d), row),
                  half(0), half(1),
                  pl.BlockSpec((2, SUBLANES, tf), lambda i, f: (0, 0, f)),
                  pl.BlockSpec((tf, d), lambda i, f: (f, 0)),
                  pl.BlockSpec((1, d), const),
                  pl.BlockSpec((1, d), const)],
        out_specs=[pl.BlockSpec((tm, d), row), pl.BlockSpec((tm, d), row)],
        out_shape=[jax.ShapeDtypeStruct((m, d), F32), jax.ShapeDtypeStruct((m, d), BF16)],
        scratch_shapes=[pltpu.VMEM((tm, d), F32), pltpu.VMEM((BF16_ROWS + tm, d), BF16)],
        compiler_params=_params("parallel", "arbitrary"),
    )(xb, xb, xres, w_ug, w_ug, c_ug, wd, gain, bias)


def _ple_kernel(xb_ref, wg_ref, p_ref, wp_ref, res_ref, xo_ref, *maybe_xob_ref):
    gate = _sigmoid(_dot(xb_ref[...], wg_ref[...]))
    proj = _dot(p_ref[...].astype(BF16), wp_ref[...])
    xn = res_ref[...] + proj * gate
    xo_ref[...] = xn
    for xob_ref in maybe_xob_ref:
        xob_ref[...] = xn.astype(BF16)


def _ple(xb, xres, p, wg, wp, *, with_bf16, tm=512):
    m, d = xb.shape
    pd = p.shape[1]
    row = lambda i: (i, 0)
    const = lambda i: (0, 0)
    out_specs = [pl.BlockSpec((tm, d), row)]
    out_shape = [jax.ShapeDtypeStruct((m, d), F32)]
    if with_bf16:
        out_specs.append(pl.BlockSpec((tm, d), row))
        out_shape.append(jax.ShapeDtypeStruct((m, d), BF16))
    out = pl.pallas_call(
        _ple_kernel,
        grid=(m // tm,),
        in_specs=[pl.BlockSpec((tm, d), row),
                  pl.BlockSpec((d, d), const),
                  pl.BlockSpec((tm, pd), row),
                  pl.BlockSpec((pd, d), const),
                  pl.BlockSpec((tm, d), row)],
        out_specs=out_specs,
        out_shape=out_shape,
        compiler_params=_params("parallel"),
    )(xb, wg, p, wp, xres)
    return (out[0], out[1]) if with_bf16 else (out[0], None)


def _ret_proj_kernel(a_ref, w_ref, cos_ref, sin_ref, o_ref, *, tn, nq, nkv):
    j = pl.program_id(1)
    acc = _dot(a_ref[...], w_ref[...])
    half = RET_QK_DIM // 2

    def rotary(scale):
        cos = cos_ref[...]
        sin = sin_ref[...]
        for h in range(tn // RET_QK_DIM):
            lo = h * RET_QK_DIM
            x1 = acc[:, lo:lo + half]
            x2 = acc[:, lo + half:lo + RET_QK_DIM]
            o_ref[:, lo:lo + half] = ((x1 * cos - x2 * sin) * scale).astype(o_ref.dtype)
            o_ref[:, lo + half:lo + RET_QK_DIM] = ((x1 * sin + x2 * cos) * scale).astype(o_ref.dtype)

    @pl.when(j < nq)
    def _():
        rotary(1.0)

    @pl.when((j >= nq) & (j < 2 * nq))
    def _():
        rotary(RET_QK_DIM ** -0.5)

    @pl.when((j >= 2 * nq) & (j < 2 * nq + nkv))
    def _():
        o_ref[...] = acc.astype(o_ref.dtype)

    @pl.when(j >= 2 * nq + nkv)
    def _():
        o_ref[...] = _silu(acc).astype(o_ref.dtype)


def _ret_proj(xb, w, cos, sin, *, seq, tm=1024, tn=2048):
    m, d = xb.shape
    n = w.shape[1]
    nq = RET_HEADS * RET_QK_DIM // tn
    nkv = RET_HEADS * RET_V_DIM // tn
    tiles_per_seq = seq // tm
    return pl.pallas_call(
        functools.partial(_ret_proj_kernel, tn=tn, nq=nq, nkv=nkv),
        grid=(m // tm, n // tn),
        in_specs=[pl.BlockSpec((tm, d), lambda i, j: (i, 0)),
                  pl.BlockSpec((d, tn), lambda i, j: (0, j)),
                  pl.BlockSpec((tm, RET_QK_DIM // 2), lambda i, j: (i % tiles_per_seq, 0)),
                  pl.BlockSpec((tm, RET_QK_DIM // 2), lambda i, j: (i % tiles_per_seq, 0))],
        out_specs=pl.BlockSpec((tm, tn), lambda i, j: (i, j)),
        out_shape=jax.ShapeDtypeStruct((m, n), BF16),
        compiler_params=_params("parallel", "arbitrary"),
    )(xb, w, cos, sin)


def _ret_core_kernel(q_ref, k_ref, v_ref, g_ref, lg_ref, gain_ref, o_ref, state_ref, *, ts):
    c = RET_CHUNK

    @pl.when(pl.program_id(2) == 0)
    def _():
        state_ref[...] = jnp.zeros_like(state_ref)

    lg = lg_ref[0]
    ii = lax.broadcasted_iota(jnp.int32, (c, c), 0)
    jj = lax.broadcasted_iota(jnp.int32, (c, c), 1)
    diff = (ii - jj).astype(F32)
    decay_mask = jnp.where(ii >= jj, jnp.exp(lg[:, :c] * jnp.maximum(diff, 0.0)), 0.0)
    row_v = lax.broadcasted_iota(jnp.int32, (c, RET_V_DIM), 0).astype(F32)
    row_k = lax.broadcasted_iota(jnp.int32, (c, RET_QK_DIM), 0).astype(F32)
    q_decay = jnp.exp(lg * (row_v + 1.0))
    k_decay = jnp.exp(lg[:, :RET_QK_DIM] * (c - 1.0 - row_k))
    chunk_decay = jnp.exp(lg * float(c))
    gain = gain_ref[...]

    chunks = range(ts // c)
    rows = [slice(n * c, (n + 1) * c) for n in chunks]
    scores = [(_dot_nt(q_ref[rows[n], :], k_ref[rows[n], :]) * decay_mask).astype(BF16)
              for n in chunks]
    inner = [_dot(scores[n], v_ref[rows[n], :]) for n in chunks]
    kv = [_dot_tn((k_ref[rows[n], :].astype(F32) * k_decay).astype(BF16), v_ref[rows[n], :])
          for n in chunks]
    state = state_ref[...]
    states = []
    for n in chunks:
        states.append(state.astype(BF16))
        state = state * chunk_decay + kv[n]
    state_ref[...] = state
    cross = [_dot(q_ref[rows[n], :], states[n]) * q_decay for n in chunks]
    for n in chunks:
        o = inner[n] + cross[n]
        mu = jnp.mean(o, axis=-1, keepdims=True)
        oc = o - mu
        var = jnp.mean(oc * oc, axis=-1, keepdims=True)
        on = oc * lax.rsqrt(var + RET_GN_EPS) * gain
        o_ref[rows[n], :] = (g_ref[rows[n], :].astype(F32) * on).astype(o_ref.dtype)


def _ret_core(proj, lg, gn_gain, *, batch, seq, ts=4096):
    m = proj.shape[0]
    nt = seq // ts
    kq = RET_HEADS
    row = lambda b, h, t: b * nt + t
    return pl.pallas_call(
        functools.partial(_ret_core_kernel, ts=ts),
        grid=(batch, RET_HEADS, nt),
        in_specs=[pl.BlockSpec((ts, RET_QK_DIM), lambda b, h, t: (row(b, h, t), h)),
                  pl.BlockSpec((ts, RET_QK_DIM), lambda b, h, t: (row(b, h, t), kq + h)),
                  pl.BlockSpec((ts, RET_V_DIM), lambda b, h, t: (row(b, h, t), kq + h)),
                  pl.BlockSpec((ts, RET_V_DIM), lambda b, h, t: (row(b, h, t), 2 * kq + h)),
                  pl.BlockSpec((1, 1, RET_V_DIM), lambda b, h, t: (h, 0, 0)),
                  pl.BlockSpec((1, RET_V_DIM), lambda b, h, t: (0, h))],
        out_specs=pl.BlockSpec((ts, RET_V_DIM), lambda b, h, t: (row(b, h, t), h)),
        out_shape=jax.ShapeDtypeStruct((m, RET_HEADS * RET_V_DIM), BF16),
        scratch_shapes=[pltpu.VMEM((RET_QK_DIM, RET_V_DIM), F32)],
        compiler_params=_params("parallel", "parallel", "arbitrary"),
    )(proj, proj, proj, proj, lg, gn_gain)


def _t5_buckets(dist):
    n = np.maximum(dist, 0)
    max_exact = REL_BUCKETS // 2
    large = max_exact + (np.log(np.maximum(n, 1) / max_exact) / np.log(REL_MAX_DIST / max_exact)
                         * (REL_BUCKETS - max_exact)).astype(np.int32)
    large = np.minimum(large, REL_BUCKETS - 1)
    return np.where(n < max_exact, n, large).astype(np.int32)


def _bias_table_kernel(rel_ref, bucket_ref, o_ref):
    h = pl.program_id(0)
    buckets = bucket_ref[...]
    table = jnp.zeros(buckets.shape, F32)
    for b in range(REL_BUCKETS):
        table = jnp.where(buckets == b, rel_ref[b, h], table)
    o_ref[0] = table


def _bias_table(rel_bias):
    c = SWA_WINDOW
    dist = np.arange(c)[:, None] + c - np.arange(2 * c)[None, :]
    buckets = jnp.asarray(_t5_buckets(dist))
    return pl.pallas_call(
        _bias_table_kernel,
        grid=(SWA_Q_HEADS,),
        in_specs=[pl.BlockSpec(memory_space=pltpu.SMEM),
                  pl.BlockSpec((c, 2 * c), lambda h: (0, 0))],
        out_specs=pl.BlockSpec((1, c, 2 * c), lambda h: (h, 0, 0)),
        out_shape=jax.ShapeDtypeStruct((SWA_Q_HEADS, c, 2 * c), F32),
        compiler_params=_params("arbitrary"),
    )(rel_bias, buckets)


def _swa_kernel(q_ref, kp_ref, kc_ref, vp_ref, vc_ref, bias_ref, sink_ref, o_ref, *, blocks_per_seq):
    c = SWA_WINDOW
    hd = SWA_HEAD_DIM
    has_prev = (pl.program_id(0) % blocks_per_seq) > 0
    ii = lax.broadcasted_iota(jnp.int32, (c, 2 * c), 0)
    jj = lax.broadcasted_iota(jnp.int32, (c, 2 * c), 1)
    ok = ((jj < c) & (jj > ii) & has_prev) | ((jj >= c) & (jj - c <= ii))
    left = lax.broadcasted_iota(jnp.int32, (c, LANES), 1) < hd
    ones = jnp.ones((2 * c, LANES), BF16)
    zeros = jnp.zeros((2 * c, hd), BF16)
    scale = hd ** -0.5
    for g in range(SWA_KV_HEADS):
        cols = slice(g * hd, (g + 1) * hd)
        k = jnp.concatenate([kp_ref[:, cols], kc_ref[:, cols]], axis=0)
        v = jnp.concatenate([vp_ref[:, cols], vc_ref[:, cols]], axis=0)
        k_half = (jnp.concatenate([k, zeros], axis=1), jnp.concatenate([zeros, k], axis=1))
        v_half = (jnp.concatenate([v, zeros], axis=1), jnp.concatenate([zeros, v], axis=1))
        heads = range(g * SWA_GROUP, (g + 1) * SWA_GROUP)
        s = [_dot_nt(q_ref[:, (h // 2) * LANES:(h // 2 + 1) * LANES], k_half[h % 2]) for h in heads]
        s = [jnp.where(ok, s[i] * scale + bias_ref[h], NEG_INF) for i, h in enumerate(heads)]
        mx = [jnp.maximum(jnp.max(s[i], axis=-1, keepdims=True), sink_ref[h])
              for i, h in enumerate(heads)]
        p = [jnp.exp(s[i] - mx[i]).astype(BF16) for i in range(SWA_GROUP)]
        den = [_dot(p[i], ones) + jnp.exp(sink_ref[h] - mx[i]) for i, h in enumerate(heads)]
        for i in range(0, SWA_GROUP, 2):
            num = _dot(p[i], v_half[0]) + _dot(p[i + 1], v_half[1])
            out = num / jnp.where(left, den[i], den[i + 1])
            lo = (g * SWA_GROUP + i) * hd
            o_ref[:, lo:lo + LANES] = out.astype(o_ref.dtype)


def _swa_core(qkv, bias, sinks, *, seq):
    m = qkv.shape[0]
    c = SWA_WINDOW
    blocks_per_seq = seq // c
    qw = SWA_Q_HEADS * SWA_HEAD_DIM
    kw = SWA_KV_HEADS * SWA_HEAD_DIM
    kcol = qw // kw
    prev = lambda r: jnp.where(r % blocks_per_seq == 0, r, r - 1)
    return pl.pallas_call(
        functools.partial(_swa_kernel, blocks_per_seq=blocks_per_seq),
        grid=(m // c,),
        in_specs=[pl.BlockSpec((c, qw), lambda r: (r, 0)),
                  pl.BlockSpec((c, kw), lambda r: (prev(r), kcol)),
                  pl.BlockSpec((c, kw), lambda r: (r, kcol)),
                  pl.BlockSpec((c, kw), lambda r: (prev(r), kcol + 1)),
                  pl.BlockSpec((c, kw), lambda r: (r, kcol + 1)),
                  pl.BlockSpec((SWA_Q_HEADS, c, 2 * c), lambda r: (0, 0, 0)),
                  pl.BlockSpec(memory_space=pltpu.SMEM)],
        out_specs=pl.BlockSpec((c, qw), lambda r: (r, 0)),
        out_shape=jax.ShapeDtypeStruct((m, qw), BF16),
        compiler_params=_params("parallel"),
    )(qkv, qkv, qkv, qkv, qkv, bias, sinks)


def _shifted_delta(x, halo_ref, at_seq_start):
    first = jnp.where(at_seq_start, jnp.zeros_like(halo_ref[SUBLANES - 1:, :]), halo_ref[SUBLANES - 1:, :])
    rows = lax.broadcasted_iota(jnp.int32, x.shape, 0)
    prev = jnp.where(rows == 0, first, pltpu.roll(x, 1, axis=0))
    return prev - x


def _rwkv_rkv_kernel(x_ref, halo_ref, mix_ref, w_ref, o_ref, amix_ref, *, tiles_per_seq, nj):
    i = pl.program_id(0)
    j = pl.program_id(1)

    @pl.when(j % nj == 0)
    def _():
        x = x_ref[...]
        xx = _shifted_delta(x, halo_ref, i % tiles_per_seq == 0)
        amix_ref[...] = (x + xx * mix_ref[0]).astype(BF16)

    o_ref[...] = _dot(amix_ref[...], w_ref[...])


def _rwkv_rkv(x, mix3, w3, *, seq, tm=512, tn=2048):
    m, d = x.shape
    nj = d // tn
    halo_blocks = tm // SUBLANES
    return pl.pallas_call(
        functools.partial(_rwkv_rkv_kernel, tiles_per_seq=seq // tm, nj=nj),
        grid=(m // tm, 3 * nj),
        in_specs=[pl.BlockSpec((tm, d), lambda i, j: (i, 0)),
                  pl.BlockSpec((SUBLANES, d), lambda i, j: (jnp.maximum(i * halo_blocks - 1, 0), 0)),
                  pl.BlockSpec((1, 1, d), lambda i, j: (j // nj, 0, 0)),
                  pl.BlockSpec((None, d, tn), lambda i, j: (j // nj, 0, j % nj))],
        out_specs=pl.BlockSpec((None, tm, tn), lambda i, j: (j // nj, i, j % nj)),
        out_shape=jax.ShapeDtypeStruct((3, m, d), F32),
        scratch_shapes=[pltpu.VMEM((tm, d), BF16)],
        compiler_params=_params("parallel", "arbitrary"),
    )(x, x, mix3, w3)


def _rwkv_lora_kernel(x_ref, halo_ref, mix_ref, w0_ref, a0_ref, w1_ref, w2_ref, a1_ref, a2_ref,
                      g1_ref, g2_ref, ld_ref, a_ref, g_ref, *, tiles_per_seq):
    i = pl.program_id(0)
    x = x_ref[...]
    xx = _shifted_delta(x, halo_ref, i % tiles_per_seq == 0)
    mix = mix_ref[...]
    xw = (x + xx * mix[0:1]).astype(BF16)
    xa = (x + xx * mix[1:2]).astype(BF16)
    xg = (x + xx * mix[2:3]).astype(BF16)
    z = w0_ref[...] + _dot(jnp.tanh(_dot(xw, w1_ref[...])).astype(BF16), w2_ref[...])
    ld_ref[...] = -math.exp(-0.5) * _sigmoid(z)
    a_ref[...] = _sigmoid(a0_ref[...] + _dot(_dot(xa, a1_ref[...]).astype(BF16), a2_ref[...]))
    g_ref[...] = _dot(_sigmoid(_dot(xg, g1_ref[...])).astype(BF16), g2_ref[...]).astype(g_ref.dtype)


def _rwkv_lora(x, mix3, w0, a0, w1, w2, a1, a2, g1, g2, *, seq, tm=512):
    m, d = x.shape
    halo_blocks = tm // SUBLANES
    row = lambda i: (i, 0)
    full = lambda a: pl.BlockSpec(a.shape, lambda i: (0, 0))
    return pl.pallas_call(
        functools.partial(_rwkv_lora_kernel, tiles_per_seq=seq // tm),
        grid=(m // tm,),
        in_specs=[pl.BlockSpec((tm, d), row),
                  pl.BlockSpec((SUBLANES, d), lambda i: (jnp.maximum(i * halo_blocks - 1, 0), 0)),
                  full(mix3), full(w0), full(a0), full(w1), full(w2), full(a1), full(a2),
                  full(g1), full(g2)],
        out_specs=[pl.BlockSpec((tm, d), row)] * 3,
        out_shape=[jax.ShapeDtypeStruct((m, d), F32), jax.ShapeDtypeStruct((m, d), F32),
                   jax.ShapeDtypeStruct((m, d), BF16)],
        compiler_params=_params("parallel"),
    )(x, x, mix3, w0, a0, w1, w2, a1, a2, g1, g2)


def _pair_mask(n):
    rows = lax.broadcasted_iota(jnp.int32, (2 * n, 2 * n), 0)
    cols = lax.broadcasted_iota(jnp.int32, (2 * n, 2 * n), 1)
    return (rows < n) == (cols < n)


def _block_diag(y, same_head):
    return jnp.where(same_head, jnp.concatenate([y, y], axis=0), 0.0).astype(BF16)


def _fold_diag(x, n, left):
    return jnp.where(left, x[:n], x[n:])


def _segment_sum(x, ones_bd):
    hi = x.astype(BF16)
    lo = (x - hi.astype(F32)).astype(BF16)
    return _dot(hi, ones_bd) + _dot(lo, ones_bd)


def _chunk_cumsum(x, n):
    pos = lax.broadcasted_iota(jnp.int32, x.shape, 0) % n
    shift = 1
    while shift < n:
        x = x + jnp.where(pos >= shift, pltpu.roll(x, shift, axis=0), 0.0)
        shift *= 2
    return x


def _rwkv_core_kernel(r_ref, k_ref, v_ref, ld_ref, a_ref, g_ref, par_ref, o_ref, state_ref, *, tc):
    hd = RWKV_HEAD_DIM
    n = RWKV_CHUNK
    assert n == hd and 2 * hd == LANES
    bf = lambda t: t.astype(BF16)

    @pl.when(pl.program_id(2) == 0)
    def _():
        state_ref[...] = jnp.zeros_like(state_ref)

    same_head = _pair_mask(n)
    ones_bd = jnp.where(same_head, 1.0, 0.0).astype(BF16)
    lane = lax.broadcasted_iota(jnp.int32, (n, LANES), 1)
    row = lax.broadcasted_iota(jnp.int32, (n, LANES), 0)
    left = lane < hd
    pos = lane % hd
    strict = row > pos
    incl = row >= pos
    diag = row == pos

    pairs = range(r_ref.shape[1] // LANES)
    par = par_ref[...]

    def prepare(p):
        lanes = slice(p * LANES, (p + 1) * LANES)
        kk_w, ka_w, rk_w = (par[i:i + 1, lanes] for i in range(3))
        r, k, v, ld, a = (ref[:, lanes] for ref in (r_ref, k_ref, v_ref, ld_ref, a_ref))
        kk = k * kk_w
        kk = kk / jnp.maximum(jnp.sqrt(_segment_sum(kk * kk, ones_bd)), 1e-12)
        km = k * (1.0 + (a - 1.0) * ka_w)
        bv = kk * a
        cw = _chunk_cumsum(ld, n)
        w_inv = jnp.exp(-cw)
        return (v, km, bv, cw, -kk * jnp.exp(cw - ld), r * jnp.exp(cw), bv * w_inv, km * w_inv,
                _segment_sum(r * km * rk_w, ones_bd) * v)

    v, km, bv, cw, a_t, r_t, b_t, k_t, bonus = (
        jnp.concatenate(parts, axis=0) for parts in zip(*[prepare(p) for p in pairs]))

    chunks = range(len(pairs) * (tc // n))
    rows = [slice(c * n, (c + 1) * n) for c in chunks]
    cw_last = [cw[(c + 1) * n - 1:(c + 1) * n] for c in chunks]
    to_end = [jnp.exp(cw_last[c] - cw[rows[c]]) for c in chunks]
    bk_e = [bf(jnp.concatenate([bv[rows[c]] * to_end[c], km[rows[c]] * to_end[c]], axis=0))
            for c in chunks]
    zeros = jnp.zeros((n, LANES), F32)
    zeros_bd = jnp.zeros((2 * n, 2 * n), BF16)
    bd = lambda t: _block_diag(t, same_head)

    x = [_dot_nt(bf(jnp.concatenate([a_t[rows[c]], r_t[rows[c]]], axis=0)),
                 jnp.concatenate([bd(b_t[rows[c]]), bd(k_t[rows[c]])], axis=0)) for c in chunks]
    a_ab = [jnp.where(strict, x[c][:n, :LANES], 0.0) for c in chunks]
    a_ak = [jnp.where(strict, x[c][:n, LANES:], 0.0) for c in chunks]
    a_rbk = [bf(jnp.concatenate([jnp.where(incl, x[c][n:, :LANES], 0.0),
                                 jnp.where(incl, x[c][n:, LANES:], 0.0)], axis=1)) for c in chunks]

    inv = [jnp.where(diag, 1.0, a_ab[c]) for c in chunks]
    power = [_dot(bf(a_ab[c]), bd(a_ab[c])) for c in chunks]
    for _ in range(int(math.log2(n)) - 2):
        both = [_dot(bf(jnp.concatenate([inv[c], power[c]], axis=0)), bd(power[c])) for c in chunks]
        inv = [inv[c] + both[c][:n] for c in chunks]
        power = [both[c][n:] for c in chunks]
    inv = [inv[c] + _dot(bf(inv[c]), bd(power[c])) for c in chunks]

    av = [_dot(bf(a_ak[c]), bd(v[rows[c]])) for c in chunks]
    au = [_dot(bf(inv[c]), jnp.concatenate([bd(a_t[rows[c]]), bd(av[c])], axis=1))
          for c in chunks]
    gc = [_dot_tn(bk_e[c], bf(jnp.concatenate(
              [au[c], jnp.concatenate([zeros, v[rows[c]]], axis=1)], axis=0))) for c in chunks]
    g_mat = [_fold_diag(gc[c][:, :LANES], n, left) + jnp.where(diag, jnp.exp(cw_last[c]), 0.0)
             for c in chunks]
    c0 = [_fold_diag(gc[c][:, LANES:], n, left) for c in chunks]
    ry = [_dot(a_rbk[c], jnp.concatenate(
              [jnp.concatenate([bd(au[c][:, :LANES]), bd(au[c][:, LANES:])], axis=1),
               jnp.concatenate([zeros_bd, bd(v[rows[c]])], axis=1)], axis=0))
          for c in chunks]
    rg = [bf(jnp.concatenate([r_t[rows[c]] + ry[c][:, :LANES], g_mat[c]], axis=0)) for c in chunks]

    nc = tc // n
    st = [state_ref[p] for p in pairs]
    ys = [[] for _ in pairs]
    for c in range(nc):
        outs = [_dot(rg[p * nc + c], bd(st[p])) for p in pairs]
        for p in pairs:
            ys[p].append(outs[p][:n] + ry[p * nc + c][:, LANES:])
            st[p] = outs[p][n:] + c0[p * nc + c]
    for p in pairs:
        state_ref[p] = st[p]

    for p in pairs:
        lanes = slice(p * LANES, (p + 1) * LANES)
        y = jnp.concatenate(ys[p], axis=0)
        mu = _segment_sum(y, ones_bd) * (1.0 / hd)
        yc = y - mu
        var = _segment_sum(yc * yc, ones_bd) * (1.0 / hd)
        yn = yc * lax.rsqrt(var + RWKV_GN_EPS) * par[3:4, lanes] + par[4:5, lanes]
        o_ref[:, lanes] = ((yn + bonus[p * tc:(p + 1) * tc]) * g_ref[:, lanes].astype(F32)
                           ).astype(o_ref.dtype)


def _rwkv_core(rkv, ld, a, g, par, *, batch, seq, tc=512, pairs=8):
    d = ld.shape[1]
    nt = seq // tc
    width = pairs * LANES
    blk = lambda b, hp, t: (b * nt + t, hp)
    sel = lambda which: pl.BlockSpec((None, tc, width), lambda b, hp, t: (which, b * nt + t, hp))
    return pl.pallas_call(
        functools.partial(_rwkv_core_kernel, tc=tc),
        grid=(batch, d // width, nt),
        in_specs=[sel(0), sel(1), sel(2),
                  pl.BlockSpec((tc, width), blk),
                  pl.BlockSpec((tc, width), blk),
                  pl.BlockSpec((tc, width), blk),
                  pl.BlockSpec((SUBLANES, width), lambda b, hp, t: (0, hp))],
        out_specs=pl.BlockSpec((tc, width), blk),
        out_shape=jax.ShapeDtypeStruct((batch * seq, d), BF16),
        scratch_shapes=[pltpu.VMEM((pairs, RWKV_HEAD_DIM, LANES), F32)],
        compiler_params=_params("parallel", "parallel", "arbitrary"),
    )(rkv, rkv, rkv, ld, a, g, par)


CAST_BLOCK_BYTES = 8 * 1024 * 1024


def _cast_rows(k, n):
    tr = k
    while tr * n * 4 > CAST_BLOCK_BYTES and tr % 2 == 0 and (tr // 2) % BF16_ROWS == 0:
        tr //= 2
    return tr


def _cast_kernel(x_ref, o_ref):
    o_ref[...] = x_ref[...].astype(o_ref.dtype)


def _to_bf16(w, layer):
    _, k, n = w.shape
    tr = _cast_rows(k, n)
    return pl.pallas_call(
        _cast_kernel,
        grid=(k // tr,),
        in_specs=[pl.BlockSpec((None, tr, n), lambda r: (layer, r, 0))],
        out_specs=pl.BlockSpec((tr, n), lambda r: (r, 0)),
        out_shape=jax.ShapeDtypeStruct((k, n), BF16),
        compiler_params=_params("parallel"),
    )(w)


def _all_to_bf16(w):
    layers, k, n = w.shape
    tr = _cast_rows(k, n)
    return pl.pallas_call(
        _cast_kernel,
        grid=(layers, k // tr),
        in_specs=[pl.BlockSpec((None, tr, n), lambda l, r: (l, r, 0))],
        out_specs=pl.BlockSpec((None, tr, n), lambda l, r: (l, r, 0)),
        out_shape=jax.ShapeDtypeStruct((layers, k, n), BF16),
        compiler_params=_params("parallel", "parallel"),
    )(w)


def _ffn_up_cast_kernel(x_ref, o_ref):
    o_ref[0, :, :D_FF] = x_ref[:, :D_FF].astype(o_ref.dtype)
    o_ref[1, :, :D_FF] = x_ref[:, D_FF:].astype(o_ref.dtype)
    o_ref[:, :, D_FF:] = jnp.zeros((2, o_ref.shape[1], D_FF_PAD - D_FF), o_ref.dtype)


def _ffn_up_to_bf16(w, layer):
    _, k, n = w.shape
    tr = _cast_rows(k, n)
    return pl.pallas_call(
        _ffn_up_cast_kernel,
        grid=(k // tr,),
        in_specs=[pl.BlockSpec((None, tr, n), lambda r: (layer, r, 0))],
        out_specs=pl.BlockSpec((2, tr, D_FF_PAD), lambda r: (0, r, 0)),
        out_shape=jax.ShapeDtypeStruct((2, k, D_FF_PAD), BF16),
        compiler_params=_params("parallel"),
    )(w)


def _ffn_down_cast_kernel(x_ref, o_ref, *, tr):
    rows = lax.broadcasted_iota(jnp.int32, x_ref.shape, 0) + pl.program_id(0) * tr
    o_ref[...] = jnp.where(rows < D_FF, x_ref[...], 0.0).astype(o_ref.dtype)


def _ffn_down_to_bf16(w, layer):
    _, _, n = w.shape
    tr = FFN_COL_TILE
    return pl.pallas_call(
        functools.partial(_ffn_down_cast_kernel, tr=tr),
        grid=(D_FF_PAD // tr,),
        in_specs=[pl.BlockSpec((None, tr, n), lambda r: (layer, r, 0))],
        out_specs=pl.BlockSpec((tr, n), lambda r: (r, 0)),
        out_shape=jax.ShapeDtypeStruct((D_FF_PAD, n), BF16),
        compiler_params=_params("parallel"),
    )(w)


def _pad_cols(w, n):
    return jnp.pad(w, ((0, 0), (0, n - w.shape[1])))


def _pad_rows(w, n):
    return jnp.pad(w, ((0, n - w.shape[0]), (0, 0)))


def kernel(x, p, ln_gain, ln_bias, ret_w_in, ret_gn_gain, ret_w_out, swa_w_qkv, swa_sinks, swa_w_out, rel_bias, rwkv_mix, rwkv_w_rkv, rwkv_w0, rwkv_w1, rwkv_w2, rwkv_a0, rwkv_a1, rwkv_a2, rwkv_g1, rwkv_g2, rwkv_k_k, rwkv_k_a, rwkv_r_k, rwkv_gn_gain, rwkv_gn_bias, rwkv_w_out, ffn_w_up, ffn_conv_w, ffn_conv_b, ffn_w_down, ple_w_proj, ple_w_gate):
    batch, seq, d = x.shape
    m = batch * seq
    xf = x.reshape(m, d)
    xb = _to_bf16(xf.reshape(1, m, d), 0)
    row = lambda v: v.reshape(1, -1)

    inv_freq = 1.0 / (ROPE_BASE ** (jnp.arange(0, RET_QK_DIM, 2, dtype=F32) / RET_QK_DIM))
    ang = jnp.arange(seq, dtype=F32)[:, None] * inv_freq[None, :]
    cos, sin = jnp.cos(ang), jnp.sin(ang)
    log_gamma = jnp.log(1.0 - 2.0 ** (-5.0 - jnp.arange(RET_HEADS, dtype=F32)))
    lg = jnp.broadcast_to(log_gamma[:, None, None], (RET_HEADS, 1, RET_V_DIM))

    for i in range(DEPTH):
        kind, j = i % N_MIXERS, i // N_MIXERS
        g1, b1 = row(ln_gain[i, 0]), row(ln_bias[i, 0])
        if kind == 0:
            proj = _ret_proj(xb, _to_bf16(ret_w_in, j), cos, sin, seq=seq)
            o = _ret_core(proj, lg, row(ret_gn_gain[j]), batch=batch, seq=seq)
            xf, xb = _matmul_ln(o, _to_bf16(ret_w_out, j), xf, g1, b1, tm=256)
        elif kind == 1:
            qkv = _matmul(xb, _to_bf16(swa_w_qkv, j), tm=1024, tn=1280, out_dtype=BF16)
            o = _swa_core(qkv, _bias_table(rel_bias), swa_sinks[j], seq=seq)
            xf, xb = _matmul_ln(o, _to_bf16(swa_w_out, j), xf, g1, b1, tm=512)
        else:
            mix = rwkv_mix[j]
            rkv = _rwkv_rkv(xf, mix[jnp.array([0, 2, 3])].reshape(3, 1, d),
                            _all_to_bf16(rwkv_w_rkv[j]), seq=seq)
            lora = RWKV_HEAD_DIM * 2
            ld, a, g = _rwkv_lora(
                xf, mix[jnp.array([1, 4, 5])], row(rwkv_w0[j]), row(rwkv_a0[j]),
                _pad_cols(rwkv_w1[j], lora).astype(BF16), _pad_rows(rwkv_w2[j], lora).astype(BF16),
                _pad_cols(rwkv_a1[j], lora).astype(BF16), _pad_rows(rwkv_a2[j], lora).astype(BF16),
                rwkv_g1[j].astype(BF16), rwkv_g2[j].astype(BF16), seq=seq)
            par = jnp.stack([rwkv_k_k[j], rwkv_k_a[j], rwkv_r_k[j].reshape(-1), rwkv_gn_gain[j],
                             rwkv_gn_bias[j]] + [jnp.zeros((d,), F32)] * 3)
            o = _rwkv_core(rkv, ld, a, g, par, batch=batch, seq=seq)
            xf, xb = _matmul_ln(o, _to_bf16(rwkv_w_out, j), xf, g1, b1, tm=512)

        conv = jnp.concatenate([ffn_conv_w[i], ffn_conv_b[i][None]], axis=0)
        conv = jnp.pad(conv, ((0, SUBLANES - conv.shape[0]), (0, 0)))
        conv = jnp.stack([_pad_cols(conv[:, :D_FF], D_FF_PAD), _pad_cols(conv[:, D_FF:], D_FF_PAD)])
        xf, xb = _ffn(xb, xf, _ffn_up_to_bf16(ffn_w_up, i), conv, _ffn_down_to_bf16(ffn_w_down, i),
                      row(ln_gain[i, 1]), row(ln_bias[i, 1]), seq=seq)
        xf, xb = _ple(xb, xf, p[i].reshape(m, PLE_DIM), _to_bf16(ple_w_gate, i),
                      _to_bf16(ple_w_proj, i), with_bf16=i + 1 < DEPTH)
    return xf.reshape(batch, seq, d)
```

```python
import functools
import math

import numpy as np
import jax
import jax.numpy as jnp
from jax import lax
from jax.experimental import pallas as pl
from jax.experimental.pallas import tpu as pltpu

F32 = jnp.float32
BF16 = jnp.bfloat16

D_MODEL = 2048
DEPTH = 4
N_MIXERS = 3

RET_HEADS = 8
RET_QK_DIM = D_MODEL // RET_HEADS
RET_V_DIM = 2 * D_MODEL // RET_HEADS
RET_CHUNK = 128
RET_GN_EPS = 1e-5
ROPE_BASE = 10000.0

SWA_HEAD_DIM = 64
SWA_Q_HEADS = D_MODEL // SWA_HEAD_DIM
SWA_KV_HEADS = SWA_Q_HEADS // 8
SWA_GROUP = SWA_Q_HEADS // SWA_KV_HEADS
SWA_WINDOW = 128
REL_BUCKETS = 32
REL_MAX_DIST = SWA_WINDOW
NEG_INF = -1e30

RWKV_HEAD_DIM = 64
RWKV_HEADS = D_MODEL // RWKV_HEAD_DIM
RWKV_GN_EPS = 64e-5
RWKV_CHUNK = 64

D_FF = 5504
CONV_WIDTH = 3
PLE_DIM = 256

LN_EPS = 1e-5
DEEPNORM_ALPHA = (2.0 * DEPTH) ** 0.25

LANES = 128
SUBLANES = 8
BF16_ROWS = 16
VMEM_LIMIT_BYTES = 56 * 1024 * 1024
FFN_COL_TILE = 512
D_FF_PAD = -(-D_FF // FFN_COL_TILE) * FFN_COL_TILE


def _params(*semantics, flags=None):
    return pltpu.CompilerParams(dimension_semantics=semantics,
                                vmem_limit_bytes=VMEM_LIMIT_BYTES, flags=flags)


def _dot(a, b):
    return jnp.dot(a, b, preferred_element_type=F32)


def _dot_nt(a, b):
    return lax.dot_general(a, b, (((1,), (1,)), ((), ())), preferred_element_type=F32)


def _dot_tn(a, b):
    return lax.dot_general(a, b, (((0,), (0,)), ((), ())), preferred_element_type=F32)


def _layer_norm_rows(z, gain, bias):
    mu = jnp.mean(z, axis=-1, keepdims=True)
    zc = z - mu
    var = jnp.mean(zc * zc, axis=-1, keepdims=True)
    return zc * lax.rsqrt(var + LN_EPS) * gain + bias


def _sigmoid(x):
    return 1.0 / (1.0 + jnp.exp(-x))


def _silu(x):
    return x * _sigmoid(x)


def _matmul_kernel(a_ref, w_ref, o_ref):
    o_ref[...] = _dot(a_ref[...], w_ref[...]).astype(o_ref.dtype)


def _matmul(a, w, *, tm, tn, out_dtype):
    m, k = a.shape
    n = w.shape[1]
    return pl.pallas_call(
        _matmul_kernel,
        grid=(m // tm, n // tn),
        in_specs=[pl.BlockSpec((tm, k), lambda i, j: (i, 0)),
                  pl.BlockSpec((k, tn), lambda i, j: (0, j))],
        out_specs=pl.BlockSpec((tm, tn), lambda i, j: (i, j)),
        out_shape=jax.ShapeDtypeStruct((m, n), out_dtype),
        compiler_params=_params("parallel", "arbitrary"),
    )(a, w)


def _matmul_ln_kernel(a_ref, w_ref, res_ref, gain_ref, bias_ref, xo_ref, xb_ref):
    y = _dot(a_ref[...], w_ref[...])
    xn = _layer_norm_rows(DEEPNORM_ALPHA * res_ref[...] + y, gain_ref[...], bias_ref[...])
    xo_ref[...] = xn
    xb_ref[...] = xn.astype(BF16)


def _matmul_ln(a, w, res, gain, bias, *, tm):
    m, k = a.shape
    d = w.shape[1]
    row = lambda i: (i, 0)
    const = lambda i: (0, 0)
    return pl.pallas_call(
        _matmul_ln_kernel,
        grid=(m // tm,),
        in_specs=[pl.BlockSpec((tm, k), row),
                  pl.BlockSpec((k, d), const),
                  pl.BlockSpec((tm, d), row),
                  pl.BlockSpec((1, d), const),
                  pl.BlockSpec((1, d), const)],
        out_specs=[pl.BlockSpec((tm, d), row), pl.BlockSpec((tm, d), row)],
        out_shape=[jax.ShapeDtypeStruct((m, d), F32), jax.ShapeDtypeStruct((m, d), BF16)],
        compiler_params=_params("parallel"),
    )(a, w, res, gain, bias)


def _ffn_kernel(xb_ref, halo_ref, res_ref, wu_ref, wg_ref, c_ref, wd_ref,
                gain_ref, bias_ref, xo_ref, xob_ref, acc_ref, x_ref,
                *, tiles_per_seq, nf):
    i = pl.program_id(0)
    f = pl.program_id(1)

    @pl.when(f == 0)
    def _():
        halo = jnp.where(i % tiles_per_seq == 0, jnp.zeros_like(halo_ref[...]), halo_ref[...])
        x_ref[...] = jnp.concatenate([halo, xb_ref[...]], axis=0)
        acc_ref[...] = jnp.zeros_like(acc_ref)

    def conv_branch(w_ref, which):
        h = _dot(x_ref[...], w_ref[...])
        c = c_ref[which]
        out = c[3:4] + h[BF16_ROWS:] * c[2:3]
        out = out + pltpu.roll(h, 1, axis=0)[BF16_ROWS:] * c[1:2]
        out = out + pltpu.roll(h, 2, axis=0)[BF16_ROWS:] * c[0:1]
        return out

    u = conv_branch(wu_ref, 0)
    gate = conv_branch(wg_ref, 1)
    acc_ref[...] += _dot((_silu(gate) * u).astype(BF16), wd_ref[...])

    @pl.when(f == nf - 1)
    def _():
        xn = _layer_norm_rows(DEEPNORM_ALPHA * res_ref[...] + acc_ref[...], gain_ref[...],
                              bias_ref[...])
        xo_ref[...] = xn
        xob_ref[...] = xn.astype(BF16)


def _ffn(xb, xres, w_ug, c_ug, wd, gain, bias, *, seq, tm=512, tf=FFN_COL_TILE):
    m, d = xb.shape
    nf = w_ug.shape[2] // tf
    halo_blocks = tm // BF16_ROWS
    row = lambda i, f: (i, 0)
    const = lambda i, f: (0, 0)
    half = lambda which: pl.BlockSpec((None, d, tf), lambda i, f: (which, 0, f))
    return pl.pallas_call(
        functools.partial(_ffn_kernel, tiles_per_seq=seq // tm, nf=nf),
        grid=(m // tm, nf),
        in_specs=[pl.BlockSpec((tm, d), row),
                  pl.BlockSpec((BF16_ROWS, d), lambda i, f: (jnp.maximum(i * halo_blocks - 1, 0), 0)),
                  pl.BlockSpec((tm, d), row),
                  half(0), half(1),
                  pl.BlockSpec((2, SUBLANES, tf), lambda i, f: (0, 0, f)),
                  pl.BlockSpec((tf, d), lambda i, f: (f, 0)),
                  pl.BlockSpec((1, d), const),
                  pl.BlockSpec((1, d), const)],
        out_specs=[pl.BlockSpec((tm, d), row), pl.BlockSpec((tm, d), row)],
        out_shape=[jax.ShapeDtypeStruct((m, d), F32), jax.ShapeDtypeStruct((m, d), BF16)],
        scratch_shapes=[pltpu.VMEM((tm, d), F32), pltpu.VMEM((BF16_ROWS + tm, d), BF16)],
        compiler_params=_params("parallel", "arbitrary"),
    )(xb, xb, xres, w_ug, w_ug, c_ug, wd, gain, bias)


def _ple_kernel(xb_ref, wg_ref, p_ref, wp_ref, res_ref, xo_ref, *maybe_xob_ref):
    gate = _sigmoid(_dot(xb_ref[...], wg_ref[...]))
    proj = _dot(p_ref[...].astype(BF16), wp_ref[...])
    xn = res_ref[...] + proj * gate
    xo_ref[...] = xn
    for xob_ref in maybe_xob_ref:
        xob_ref[...] = xn.astype(BF16)


def _ple(xb, xres, p, wg, wp, *, with_bf16, tm=512):
    m, d = xb.shape
    pd = p.shape[1]
    row = lambda i: (i, 0)
    const = lambda i: (0, 0)
    out_specs = [pl.BlockSpec((tm, d), row)]
    out_shape = [jax.ShapeDtypeStruct((m, d), F32)]
    if with_bf16:
        out_specs.append(pl.BlockSpec((tm, d), row))
        out_shape.append(jax.ShapeDtypeStruct((m, d), BF16))
    out = pl.pallas_call(
        _ple_kernel,
        grid=(m // tm,),
        in_specs=[pl.BlockSpec((tm, d), row),
                  pl.BlockSpec((d, d), const),
                  pl.BlockSpec((tm, pd), row),
                  pl.BlockSpec((pd, d), const),
                  pl.BlockSpec((tm, d), row)],
        out_specs=out_specs,
        out_shape=out_shape,
        compiler_params=_params("parallel"),
    )(xb, wg, p, wp, xres)
    return (out[0], out[1]) if with_bf16 else (out[0], None)


def _ret_proj_kernel(a_ref, w_ref, cos_ref, sin_ref, o_ref, *maybe_a_bf16_ref, tn, nq, nkv):
    j = pl.program_id(1)
    if maybe_a_bf16_ref:
        a_bf16_ref, = maybe_a_bf16_ref

        @pl.when(j == 0)
        def _():
            a_bf16_ref[...] = a_ref[...].astype(BF16)

        a_ref = a_bf16_ref
    acc = _dot(a_ref[...], w_ref[...])
    half = RET_QK_DIM // 2

    def rotary(scale):
        cos = cos_ref[...]
        sin = sin_ref[...]
        for h in range(tn // RET_QK_DIM):
            lo = h * RET_QK_DIM
            x1 = acc[:, lo:lo + half]
            x2 = acc[:, lo + half:lo + RET_QK_DIM]
            o_ref[:, lo:lo + half] = ((x1 * cos - x2 * sin) * scale).astype(o_ref.dtype)
            o_ref[:, lo + half:lo + RET_QK_DIM] = ((x1 * sin + x2 * cos) * scale).astype(o_ref.dtype)

    @pl.when(j < nq)
    def _():
        rotary(1.0)

    @pl.when((j >= nq) & (j < 2 * nq))
    def _():
        rotary(RET_QK_DIM ** -0.5)

    @pl.when((j >= 2 * nq) & (j < 2 * nq + nkv))
    def _():
        o_ref[...] = acc.astype(o_ref.dtype)

    @pl.when(j >= 2 * nq + nkv)
    def _():
        o_ref[...] = _silu(acc).astype(o_ref.dtype)


def _ret_proj(xb, w, cos, sin, *, seq, tm=1024, tn=2048):
    m, d = xb.shape
    n = w.shape[1]
    scratch = [] if xb.dtype == BF16 else [pltpu.VMEM((tm, d), BF16)]
    nq = RET_HEADS * RET_QK_DIM // tn
    nkv = RET_HEADS * RET_V_DIM // tn
    tiles_per_seq = seq // tm
    return pl.pallas_call(
        functools.partial(_ret_proj_kernel, tn=tn, nq=nq, nkv=nkv),
        grid=(m // tm, n // tn),
        in_specs=[pl.BlockSpec((tm, d), lambda i, j: (i, 0)),
                  pl.BlockSpec((d, tn), lambda i, j: (0, j)),
                  pl.BlockSpec((tm, RET_QK_DIM // 2), lambda i, j: (i % tiles_per_seq, 0)),
                  pl.BlockSpec((tm, RET_QK_DIM // 2), lambda i, j: (i % tiles_per_seq, 0))],
        out_specs=pl.BlockSpec((tm, tn), lambda i, j: (i, j)),
        out_shape=jax.ShapeDtypeStruct((m, n), BF16),
        scratch_shapes=scratch,
        compiler_params=_params("parallel", "arbitrary"),
    )(xb, w, cos, sin)


def _ret_core_kernel(q_ref, k_ref, v_ref, g_ref, lg_ref, gain_ref, o_ref, state_ref, *, ts):
    c = RET_CHUNK

    @pl.when(pl.program_id(2) == 0)
    def _():
        state_ref[...] = jnp.zeros_like(state_ref)

    lg = lg_ref[0]
    ii = lax.broadcasted_iota(jnp.int32, (c, c), 0)
    jj = lax.broadcasted_iota(jnp.int32, (c, c), 1)
    diff = (ii - jj).astype(F32)
    decay_mask = jnp.where(ii >= jj, jnp.exp(lg[:, :c] * jnp.maximum(diff, 0.0)), 0.0)
    row_v = lax.broadcasted_iota(jnp.int32, (c, RET_V_DIM), 0).astype(F32)
    row_k = lax.broadcasted_iota(jnp.int32, (c, RET_QK_DIM), 0).astype(F32)
    q_decay = jnp.exp(lg * (row_v + 1.0))
    k_decay = jnp.exp(lg[:, :RET_QK_DIM] * (c - 1.0 - row_k))
    chunk_decay = jnp.exp(lg * float(c))
    gain = gain_ref[...]

    chunks = range(ts // c)
    rows = [slice(n * c, (n + 1) * c) for n in chunks]
    scores = [(_dot_nt(q_ref[rows[n], :], k_ref[rows[n], :]) * decay_mask).astype(BF16)
              for n in chunks]
    inner = [_dot(scores[n], v_ref[rows[n], :]) for n in chunks]
    kv = [_dot_tn((k_ref[rows[n], :].astype(F32) * k_decay).astype(BF16), v_ref[rows[n], :])
          for n in chunks]
    state = state_ref[...]
    states = []
    for n in chunks:
        states.append(state.astype(BF16))
        state = state * chunk_decay + kv[n]
    state_ref[...] = state
    cross = [_dot(q_ref[rows[n], :], states[n]) * q_decay for n in chunks]
    for n in chunks:
        o = inner[n] + cross[n]
        mu = jnp.mean(o, axis=-1, keepdims=True)
        oc = o - mu
        var = jnp.mean(oc * oc, axis=-1, keepdims=True)
        on = oc * lax.rsqrt(var + RET_GN_EPS) * gain
        o_ref[rows[n], :] = (g_ref[rows[n], :].astype(F32) * on).astype(o_ref.dtype)


def _ret_core(proj, lg, gn_gain, *, batch, seq, ts=4096):
    m = proj.shape[0]
    nt = seq // ts
    kq = RET_HEADS
    row = lambda b, h, t: b * nt + t
    return pl.pallas_call(
        functools.partial(_ret_core_kernel, ts=ts),
        grid=(batch, RET_HEADS, nt),
        in_specs=[pl.BlockSpec((ts, RET_QK_DIM), lambda b, h, t: (row(b, h, t), h)),
                  pl.BlockSpec((ts, RET_QK_DIM), lambda b, h, t: (row(b, h, t), kq + h)),
                  pl.BlockSpec((ts, RET_V_DIM), lambda b, h, t: (row(b, h, t), kq + h)),
                  pl.BlockSpec((ts, RET_V_DIM), lambda b, h, t: (row(b, h, t), 2 * kq + h)),
                  pl.BlockSpec((1, 1, RET_V_DIM), lambda b, h, t: (h, 0, 0)),
                  pl.BlockSpec((1, RET_V_DIM), lambda b, h, t: (0, h))],
        out_specs=pl.BlockSpec((ts, RET_V_DIM), lambda b, h, t: (row(b, h, t), h)),
        out_shape=jax.ShapeDtypeStruct((m, RET_HEADS * RET_V_DIM), BF16),
        scratch_shapes=[pltpu.VMEM((RET_QK_DIM, RET_V_DIM), F32)],
        compiler_params=_params("parallel", "parallel", "arbitrary"),
    )(proj, proj, proj, proj, lg, gn_gain)


def _t5_buckets(dist):
    n = np.maximum(dist, 0)
    max_exact = REL_BUCKETS // 2
    large = max_exact + (np.log(np.maximum(n, 1) / max_exact) / np.log(REL_MAX_DIST / max_exact)
                         * (REL_BUCKETS - max_exact)).astype(np.int32)
    large = np.minimum(large, REL_BUCKETS - 1)
    return np.where(n < max_exact, n, large).astype(np.int32)


def _bias_table_kernel(rel_ref, bucket_ref, o_ref):
    h = pl.program_id(0)
    buckets = bucket_ref[...]
    table = jnp.zeros(buckets.shape, F32)
    for b in range(REL_BUCKETS):
        table = jnp.where(buckets == b, rel_ref[b, h], table)
    o_ref[0] = table


def _bias_table(rel_bias):
    c = SWA_WINDOW
    dist = np.arange(c)[:, None] + c - np.arange(2 * c)[None, :]
    buckets = jnp.asarray(_t5_buckets(dist))
    return pl.pallas_call(
        _bias_table_kernel,
        grid=(SWA_Q_HEADS,),
        in_specs=[pl.BlockSpec(memory_space=pltpu.SMEM),
                  pl.BlockSpec((c, 2 * c), lambda h: (0, 0))],
        out_specs=pl.BlockSpec((1, c, 2 * c), lambda h: (h, 0, 0)),
        out_shape=jax.ShapeDtypeStruct((SWA_Q_HEADS, c, 2 * c), F32),
        compiler_params=_params("arbitrary"),
    )(rel_bias, buckets)


def _swa_kernel(q_ref, kp_ref, kc_ref, vp_ref, vc_ref, bias_ref, sink_ref, o_ref, *, blocks_per_seq):
    c = SWA_WINDOW
    hd = SWA_HEAD_DIM
    has_prev = (pl.program_id(0) % blocks_per_seq) > 0
    ii = lax.broadcasted_iota(jnp.int32, (c, 2 * c), 0)
    jj = lax.broadcasted_iota(jnp.int32, (c, 2 * c), 1)
    ok = ((jj < c) & (jj > ii) & has_prev) | ((jj >= c) & (jj - c <= ii))
    left = lax.broadcasted_iota(jnp.int32, (c, LANES), 1) < hd
    ones = jnp.ones((2 * c, LANES), BF16)
    zeros = jnp.zeros((2 * c, hd), BF16)
    scale = hd ** -0.5
    for g in range(SWA_KV_HEADS):
        cols = slice(g * hd, (g + 1) * hd)
        k = jnp.concatenate([kp_ref[:, cols], kc_ref[:, cols]], axis=0)
        v = jnp.concatenate([vp_ref[:, cols], vc_ref[:, cols]], axis=0)
        k_half = (jnp.concatenate([k, zeros], axis=1), jnp.concatenate([zeros, k], axis=1))
        v_half = (jnp.concatenate([v, zeros], axis=1), jnp.concatenate([zeros, v], axis=1))
        heads = range(g * SWA_GROUP, (g + 1) * SWA_GROUP)
        s = [_dot_nt(q_ref[:, (h // 2) * LANES:(h // 2 + 1) * LANES], k_half[h % 2]) for h in heads]
        s = [jnp.where(ok, s[i] * scale + bias_ref[h], NEG_INF) for i, h in enumerate(heads)]
        mx = [jnp.maximum(jnp.max(s[i], axis=-1, keepdims=True), sink_ref[h])
              for i, h in enumerate(heads)]
        p = [jnp.exp(s[i] - mx[i]).astype(BF16) for i in range(SWA_GROUP)]
        den = [_dot(p[i], ones) + jnp.exp(sink_ref[h] - mx[i]) for i, h in enumerate(heads)]
        for i in range(0, SWA_GROUP, 2):
            num = _dot(p[i], v_half[0]) + _dot(p[i + 1], v_half[1])
            out = num / jnp.where(left, den[i], den[i + 1])
            lo = (g * SWA_GROUP + i) * hd
            o_ref[:, lo:lo + LANES] = out.astype(o_ref.dtype)


def _swa_core(qkv, bias, sinks, *, seq):
    m = qkv.shape[0]
    c = SWA_WINDOW
    blocks_per_seq = seq // c
    qw = SWA_Q_HEADS * SWA_HEAD_DIM
    kw = SWA_KV_HEADS * SWA_HEAD_DIM
    kcol = qw // kw
    prev = lambda r: jnp.where(r % blocks_per_seq == 0, r, r - 1)
    return pl.pallas_call(
        functools.partial(_swa_kernel, blocks_per_seq=blocks_per_seq),
        grid=(m // c,),
        in_specs=[pl.BlockSpec((c, qw), lambda r: (r, 0)),
                  pl.BlockSpec((c, kw), lambda r: (prev(r), kcol)),
                  pl.BlockSpec((c, kw), lambda r: (r, kcol)),
                  pl.BlockSpec((c, kw), lambda r: (prev(r), kcol + 1)),
                  pl.BlockSpec((c, kw), lambda r: (r, kcol + 1)),
                  pl.BlockSpec((SWA_Q_HEADS, c, 2 * c), lambda r: (0, 0, 0)),
                  pl.BlockSpec(memory_space=pltpu.SMEM)],
        out_specs=pl.BlockSpec((c, qw), lambda r: (r, 0)),
        out_shape=jax.ShapeDtypeStruct((m, qw), BF16),
        compiler_params=_params("parallel"),
    )(qkv, qkv, qkv, qkv, qkv, bias, sinks)


def _shifted_delta(x, halo_ref, at_seq_start):
    first = jnp.where(at_seq_start, jnp.zeros_like(halo_ref[SUBLANES - 1:, :]), halo_ref[SUBLANES - 1:, :])
    rows = lax.broadcasted_iota(jnp.int32, x.shape, 0)
    prev = jnp.where(rows == 0, first, pltpu.roll(x, 1, axis=0))
    return prev - x


def _rwkv_rkv_kernel(x_ref, halo_ref, mix_ref, w_ref, o_ref, amix_ref, *, tiles_per_seq, nj):
    i = pl.program_id(0)
    j = pl.program_id(1)

    @pl.when(j % nj == 0)
    def _():
        x = x_ref[...]
        xx = _shifted_delta(x, halo_ref, i % tiles_per_seq == 0)
        amix_ref[...] = (x + xx * mix_ref[0]).astype(BF16)

    o_ref[...] = _dot(amix_ref[...], w_ref[...])


def _rwkv_rkv(x, mix3, w3, *, seq, tm=512, tn=2048):
    m, d = x.shape
    nj = d // tn
    halo_blocks = tm // SUBLANES
    return pl.pallas_call(
        functools.partial(_rwkv_rkv_kernel, tiles_per_seq=seq // tm, nj=nj),
        grid=(m // tm, 3 * nj),
        in_specs=[pl.BlockSpec((tm, d), lambda i, j: (i, 0)),
                  pl.BlockSpec((SUBLANES, d), lambda i, j: (jnp.maximum(i * halo_blocks - 1, 0), 0)),
                  pl.BlockSpec((1, 1, d), lambda i, j: (j // nj, 0, 0)),
                  pl.BlockSpec((None, d, tn), lambda i, j: (j // nj, 0, j % nj))],
        out_specs=pl.BlockSpec((None, tm, tn), lambda i, j: (j // nj, i, j % nj)),
        out_shape=jax.ShapeDtypeStruct((3, m, d), F32),
        scratch_shapes=[pltpu.VMEM((tm, d), BF16)],
        compiler_params=_params("parallel", "arbitrary"),
    )(x, x, mix3, w3)


def _rwkv_lora_kernel(x_ref, halo_ref, mix_ref, w0_ref, a0_ref, w1_ref, w2_ref, a1_ref, a2_ref,
                      g1_ref, g2_ref, ld_ref, a_ref, g_ref, *, tiles_per_seq):
    i = pl.program_id(0)
    x = x_ref[...]
    xx = _shifted_delta(x, halo_ref, i % tiles_per_seq == 0)
    mix = mix_ref[...]
    xw = (x + xx * mix[0:1]).astype(BF16)
    xa = (x + xx * mix[1:2]).astype(BF16)
    xg = (x + xx * mix[2:3]).astype(BF16)
    z = w0_ref[...] + _dot(jnp.tanh(_dot(xw, w1_ref[...])).astype(BF16), w2_ref[...])
    ld_ref[...] = -math.exp(-0.5) * _sigmoid(z)
    a_ref[...] = _sigmoid(a0_ref[...] + _dot(_dot(xa, a1_ref[...]).astype(BF16), a2_ref[...]))
    g_ref[...] = _dot(_sigmoid(_dot(xg, g1_ref[...])).astype(BF16), g2_ref[...]).astype(g_ref.dtype)


def _rwkv_lora(x, mix3, w0, a0, w1, w2, a1, a2, g1, g2, *, seq, tm=512):
    m, d = x.shape
    halo_blocks = tm // SUBLANES
    row = lambda i: (i, 0)
    full = lambda a: pl.BlockSpec(a.shape, lambda i: (0, 0))
    return pl.pallas_call(
        functools.partial(_rwkv_lora_kernel, tiles_per_seq=seq // tm),
        grid=(m // tm,),
        in_specs=[pl.BlockSpec((tm, d), row),
                  pl.BlockSpec((SUBLANES, d), lambda i: (jnp.maximum(i * halo_blocks - 1, 0), 0)),
                  full(mix3), full(w0), full(a0), full(w1), full(w2), full(a1), full(a2),
                  full(g1), full(g2)],
        out_specs=[pl.BlockSpec((tm, d), row)] * 3,
        out_shape=[jax.ShapeDtypeStruct((m, d), F32), jax.ShapeDtypeStruct((m, d), F32),
                   jax.ShapeDtypeStruct((m, d), BF16)],
        compiler_params=_params("parallel"),
    )(x, x, mix3, w0, a0, w1, w2, a1, a2, g1, g2)


def _pair_mask(n):
    rows = lax.broadcasted_iota(jnp.int32, (2 * n, 2 * n), 0)
    cols = lax.broadcasted_iota(jnp.int32, (2 * n, 2 * n), 1)
    return (rows < n) == (cols < n)


def _block_diag(y, same_head):
    return jnp.where(same_head, jnp.concatenate([y, y], axis=0), 0.0).astype(BF16)


def _fold_diag(x, n, left):
    return jnp.where(left, x[:n], x[n:])


def _segment_sum(x, ones_bd):
    hi = x.astype(BF16)
    lo = (x - hi.astype(F32)).astype(BF16)
    return _dot(hi, ones_bd) + _dot(lo, ones_bd)


def _chunk_cumsum(x, n):
    pos = lax.broadcasted_iota(jnp.int32, x.shape, 0) % n
    shift = 1
    while shift < n:
        x = x + jnp.where(pos >= shift, pltpu.roll(x, shift, axis=0), 0.0)
        shift *= 2
    return x


def _rwkv_core_kernel(r_ref, k_ref, v_ref, ld_ref, a_ref, g_ref, par_ref, o_ref, state_ref, *, tc):
    hd = RWKV_HEAD_DIM
    n = RWKV_CHUNK
    assert n == hd and 2 * hd == LANES
    bf = lambda t: t.astype(BF16)

    @pl.when(pl.program_id(2) == 0)
    def _():
        state_ref[...] = jnp.zeros_like(state_ref)

    same_head = _pair_mask(n)
    ones_bd = jnp.where(same_head, 1.0, 0.0).astype(BF16)
    lane = lax.broadcasted_iota(jnp.int32, (n, LANES), 1)
    row = lax.broadcasted_iota(jnp.int32, (n, LANES), 0)
    left = lane < hd
    pos = lane % hd
    strict = row > pos
    incl = row >= pos
    diag = row == pos

    pairs = range(r_ref.shape[1] // LANES)
    par = par_ref[...]

    def prepare(p):
        lanes = slice(p * LANES, (p + 1) * LANES)
        kk_w, ka_w, rk_w = (par[i:i + 1, lanes] for i in range(3))
        r, k, v, ld, a = (ref[:, lanes] for ref in (r_ref, k_ref, v_ref, ld_ref, a_ref))
        kk = k * kk_w
        kk = kk / jnp.maximum(jnp.sqrt(_segment_sum(kk * kk, ones_bd)), 1e-12)
        km = k * (1.0 + (a - 1.0) * ka_w)
        bv = kk * a
        cw = _chunk_cumsum(ld, n)
        w_inv = jnp.exp(-cw)
        return (v, km, bv, cw, -kk * jnp.exp(cw - ld), r * jnp.exp(cw), bv * w_inv, km * w_inv,
                _segment_sum(r * km * rk_w, ones_bd) * v)

    v, km, bv, cw, a_t, r_t, b_t, k_t, bonus = (
        jnp.concatenate(parts, axis=0) for parts in zip(*[prepare(p) for p in pairs]))

    chunks = range(len(pairs) * (tc // n))
    rows = [slice(c * n, (c + 1) * n) for c in chunks]
    cw_last = [cw[(c + 1) * n - 1:(c + 1) * n] for c in chunks]
    to_end = [jnp.exp(cw_last[c] - cw[rows[c]]) for c in chunks]
    bk_e = [bf(jnp.concatenate([bv[rows[c]] * to_end[c], km[rows[c]] * to_end[c]], axis=0))
            for c in chunks]
    zeros = jnp.zeros((n, LANES), F32)
    zeros_bd = jnp.zeros((2 * n, 2 * n), BF16)
    bd = lambda t: _block_diag(t, same_head)

    x = [_dot_nt(bf(jnp.concatenate([a_t[rows[c]], r_t[rows[c]]], axis=0)),
                 jnp.concatenate([bd(b_t[rows[c]]), bd(k_t[rows[c]])], axis=0)) for c in chunks]
    a_ab = [jnp.where(strict, x[c][:n, :LANES], 0.0) for c in chunks]
    a_ak = [jnp.where(strict, x[c][:n, LANES:], 0.0) for c in chunks]
    a_rbk = [bf(jnp.concatenate([jnp.where(incl, x[c][n:, :LANES], 0.0),
                                 jnp.where(incl, x[c][n:, LANES:], 0.0)], axis=1)) for c in chunks]

    inv = [jnp.where(diag, 1.0, a_ab[c]) for c in chunks]
    power = [_dot(bf(a_ab[c]), bd(a_ab[c])) for c in chunks]
    for _ in range(int(math.log2(n)) - 2):
        both = [_dot(bf(jnp.concatenate([inv[c], power[c]], axis=0)), bd(power[c])) for c in chunks]
        inv = [inv[c] + both[c][:n] for c in chunks]
        power = [both[c][n:] for c in chunks]
    inv = [inv[c] + _dot(bf(inv[c]), bd(power[c])) for c in chunks]

    av = [_dot(bf(a_ak[c]), bd(v[rows[c]])) for c in chunks]
    au = [_dot(bf(inv[c]), jnp.concatenate([bd(a_t[rows[c]]), bd(av[c])], axis=1))
          for c in chunks]
    gc = [_dot_tn(bk_e[c], bf(jnp.concatenate(
              [au[c], jnp.concatenate([zeros, v[rows[c]]], axis=1)], axis=0))) for c in chunks]
    g_mat = [_fold_diag(gc[c][:, :LANES], n, left) + jnp.where(diag, jnp.exp(cw_last[c]), 0.0)
             for c in chunks]
    c0 = [_fold_diag(gc[c][:, LANES:], n, left) for c in chunks]
    ry = [_dot(a_rbk[c], jnp.concatenate(
              [jnp.concatenate([bd(au[c][:, :LANES]), bd(au[c][:, LANES:])], axis=1),
               jnp.concatenate([zeros_bd, bd(v[rows[c]])], axis=1)], axis=0))
          for c in chunks]
    rg = [bf(jnp.concatenate([r_t[rows[c]] + ry[c][:, :LANES], g_mat[c]], axis=0)) for c in chunks]

    nc = tc // n
    st = [state_ref[p] for p in pairs]
    ys = [[] for _ in pairs]
    for c in range(nc):
        outs = [_dot(rg[p * nc + c], bd(st[p])) for p in pairs]
        for p in pairs:
            ys[p].append(outs[p][:n] + ry[p * nc + c][:, LANES:])
            st[p] = outs[p][n:] + c0[p * nc + c]
    for p in pairs:
        state_ref[p] = st[p]

    for p in pairs:
        lanes = slice(p * LANES, (p + 1) * LANES)
        y = jnp.concatenate(ys[p], axis=0)
        mu = _segment_sum(y, ones_bd) * (1.0 / hd)
        yc = y - mu
        var = _segment_sum(yc * yc, ones_bd) * (1.0 / hd)
        yn = yc * lax.rsqrt(var + RWKV_GN_EPS) * par[3:4, lanes] + par[4:5, lanes]
        o_ref[:, lanes] = ((yn + bonus[p * tc:(p + 1) * tc]) * g_ref[:, lanes].astype(F32)
                           ).astype(o_ref.dtype)


def _rwkv_core(rkv, ld, a, g, par, *, batch, seq, tc=256, pairs=16):
    d = ld.shape[1]
    nt = seq // tc
    width = pairs * LANES
    blk = lambda b, hp, t: (b * nt + t, hp)
    sel = lambda which: pl.BlockSpec((None, tc, width), lambda b, hp, t: (which, b * nt + t, hp))
    return pl.pallas_call(
        functools.partial(_rwkv_core_kernel, tc=tc),
        grid=(batch, d // width, nt),
        in_specs=[sel(0), sel(1), sel(2),
                  pl.BlockSpec((tc, width), blk),
                  pl.BlockSpec((tc, width), blk),
                  pl.BlockSpec((tc, width), blk),
                  pl.BlockSpec((SUBLANES, width), lambda b, hp, t: (0, hp))],
        out_specs=pl.BlockSpec((tc, width), blk),
        out_shape=jax.ShapeDtypeStruct((batch * seq, d), BF16),
        scratch_shapes=[pltpu.VMEM((pairs, RWKV_HEAD_DIM, LANES), F32)],
        compiler_params=_params("parallel", "parallel", "arbitrary"),
    )(rkv, rkv, rkv, ld, a, g, par)


CAST_BLOCK_BYTES = 8 * 1024 * 1024


def _cast_rows(k, n):
    tr = k
    while tr * n * 4 > CAST_BLOCK_BYTES and tr % 2 == 0 and (tr // 2) % BF16_ROWS == 0:
        tr //= 2
    return tr


def _cast_kernel(x_ref, o_ref):
    o_ref[...] = x_ref[...].astype(o_ref.dtype)


def _to_bf16(w, layer):
    _, k, n = w.shape
    tr = _cast_rows(k, n)
    return pl.pallas_call(
        _cast_kernel,
        grid=(k // tr,),
        in_specs=[pl.BlockSpec((None, tr, n), lambda r: (layer, r, 0))],
        out_specs=pl.BlockSpec((tr, n), lambda r: (r, 0)),
        out_shape=jax.ShapeDtypeStruct((k, n), BF16),
        compiler_params=_params("parallel"),
    )(w)


def _all_to_bf16(w):
    layers, k, n = w.shape
    tr = _cast_rows(k, n)
    return pl.pallas_call(
        _cast_kernel,
        grid=(layers, k // tr),
        in_specs=[pl.BlockSpec((None, tr, n), lambda l, r: (l, r, 0))],
        out_specs=pl.BlockSpec((None, tr, n), lambda l, r: (l, r, 0)),
        out_shape=jax.ShapeDtypeStruct((layers, k, n), BF16),
        compiler_params=_params("parallel", "parallel"),
    )(w)


def _ffn_up_cast_kernel(x_ref, o_ref):
    o_ref[0, :, :D_FF] = x_ref[:, :D_FF].astype(o_ref.dtype)
    o_ref[1, :, :D_FF] = x_ref[:, D_FF:].astype(o_ref.dtype)
    o_ref[:, :, D_FF:] = jnp.zeros((2, o_ref.shape[1], D_FF_PAD - D_FF), o_ref.dtype)


def _ffn_up_to_bf16(w, layer):
    _, k, n = w.shape
    tr = _cast_rows(k, n)
    return pl.pallas_call(
        _ffn_up_cast_kernel,
        grid=(k // tr,),
        in_specs=[pl.BlockSpec((None, tr, n), lambda r: (layer, r, 0))],
        out_specs=pl.BlockSpec((2, tr, D_FF_PAD), lambda r: (0, r, 0)),
        out_shape=jax.ShapeDtypeStruct((2, k, D_FF_PAD), BF16),
        compiler_params=_params("parallel"),
    )(w)


def _ffn_down_cast_kernel(x_ref, o_ref, *, tr):
    rows = lax.broadcasted_iota(jnp.int32, x_ref.shape, 0) + pl.program_id(0) * tr
    o_ref[...] = jnp.where(rows < D_FF, x_ref[...], 0.0).astype(o_ref.dtype)


def _ffn_down_to_bf16(w, layer):
    _, _, n = w.shape
    tr = FFN_COL_TILE
    return pl.pallas_call(
        functools.partial(_ffn_down_cast_kernel, tr=tr),
        grid=(D_FF_PAD // tr,),
        in_specs=[pl.BlockSpec((None, tr, n), lambda r: (layer, r, 0))],
        out_specs=pl.BlockSpec((tr, n), lambda r: (r, 0)),
        out_shape=jax.ShapeDtypeStruct((D_FF_PAD, n), BF16),
        compiler_params=_params("parallel"),
    )(w)


def _pad_cols(w, n):
    return jnp.pad(w, ((0, 0), (0, n - w.shape[1])))


def _pad_rows(w, n):
    return jnp.pad(w, ((0, n - w.shape[0]), (0, 0)))


def kernel(x, p, ln_gain, ln_bias, ret_w_in, ret_gn_gain, ret_w_out, swa_w_qkv, swa_sinks, swa_w_out, rel_bias, rwkv_mix, rwkv_w_rkv, rwkv_w0, rwkv_w1, rwkv_w2, rwkv_a0, rwkv_a1, rwkv_a2, rwkv_g1, rwkv_g2, rwkv_k_k, rwkv_k_a, rwkv_r_k, rwkv_gn_gain, rwkv_gn_bias, rwkv_w_out, ffn_w_up, ffn_conv_w, ffn_conv_b, ffn_w_down, ple_w_proj, ple_w_gate):
    batch, seq, d = x.shape
    m = batch * seq
    xf = x.reshape(m, d)
    xb = xf
    row = lambda v: v.reshape(1, -1)

    inv_freq = 1.0 / (ROPE_BASE ** (jnp.arange(0, RET_QK_DIM, 2, dtype=F32) / RET_QK_DIM))
    ang = jnp.arange(seq, dtype=F32)[:, None] * inv_freq[None, :]
    cos, sin = jnp.cos(ang), jnp.sin(ang)
    log_gamma = jnp.log(1.0 - 2.0 ** (-5.0 - jnp.arange(RET_HEADS, dtype=F32)))
    lg = jnp.broadcast_to(log_gamma[:, None, None], (RET_HEADS, 1, RET_V_DIM))

    for i in range(DEPTH):
        kind, j = i % N_MIXERS, i // N_MIXERS
        g1, b1 = row(ln_gain[i, 0]), row(ln_bias[i, 0])
        if kind == 0:
            proj = _ret_proj(xb, _to_bf16(ret_w_in, j), cos, sin, seq=seq)
            o = _ret_core(proj, lg, row(ret_gn_gain[j]), batch=batch, seq=seq)
            xf, xb = _matmul_ln(o, _to_bf16(ret_w_out, j), xf, g1, b1, tm=256)
        elif kind == 1:
            qkv = _matmul(xb, _to_bf16(swa_w_qkv, j), tm=1024, tn=1280, out_dtype=BF16)
            o = _swa_core(qkv, _bias_table(rel_bias), swa_sinks[j], seq=seq)
            xf, xb = _matmul_ln(o, _to_bf16(swa_w_out, j), xf, g1, b1, tm=512)
        else:
            mix = rwkv_mix[j]
            rkv = _rwkv_rkv(xf, mix[jnp.array([0, 2, 3])].reshape(3, 1, d),
                            _all_to_bf16(rwkv_w_rkv[j]), seq=seq)
            lora = RWKV_HEAD_DIM * 2
            ld, a, g = _rwkv_lora(
                xf, mix[jnp.array([1, 4, 5])], row(rwkv_w0[j]), row(rwkv_a0[j]),
                _pad_cols(rwkv_w1[j], lora).astype(BF16), _pad_rows(rwkv_w2[j], lora).astype(BF16),
                _pad_cols(rwkv_a1[j], lora).astype(BF16), _pad_rows(rwkv_a2[j], lora).astype(BF16),
                rwkv_g1[j].astype(BF16), rwkv_g2[j].astype(BF16), seq=seq)
            par = jnp.stack([rwkv_k_k[j], rwkv_k_a[j], rwkv_r_k[j].reshape(-1), rwkv_gn_gain[j],
                             rwkv_gn_bias[j]] + [jnp.zeros((d,), F32)] * 3)
            o = _rwkv_core(rkv, ld, a, g, par, batch=batch, seq=seq)
            xf, xb = _matmul_ln(o, _to_bf16(rwkv_w_out, j), xf, g1, b1, tm=512)

        conv = jnp.concatenate([ffn_conv_w[i], ffn_conv_b[i][None]], axis=0)
        conv = jnp.pad(conv, ((0, SUBLANES - conv.shape[0]), (0, 0)))
        conv = jnp.stack([_pad_cols(conv[:, :D_FF], D_FF_PAD), _pad_cols(conv[:, D_FF:], D_FF_PAD)])
        xf, xb = _ffn(xb, xf, _ffn_up_to_bf16(ffn_w_up, i), conv, _ffn_down_to_bf16(ffn_w_down, i),
                      row(ln_gain[i, 1]), row(ln_bias[i, 1]), seq=seq)
        xf, xb = _ple(xb, xf, p[i].reshape(m, PLE_DIM), _to_bf16(ple_w_gate, i),
                      _to_bf16(ple_w_proj, i), with_bf16=i + 1 < DEPTH)
    return xf.reshape(batch, seq, d)
```
